```python
import jax, jax.numpy as jnp
from jax import lax
import numpy as np

D_MODEL = 1024
BATCH = 8
SEQ = 2048
DEPTH = 4
DEC_BATCH = 128
DEC_SEQ = 8
PAST_LEN = 8192
PAGE_SIZE = 128

N_MIXERS = 2
N_A_LAYERS = (DEPTH + N_MIXERS - 1) // N_MIXERS
N_B_LAYERS = DEPTH // N_MIXERS
N_SUB = 3
HALF_STEP = 0.5
EPS = 1e-6
N_HEADS = 8
NOPE_DIM = 128
ROPE_DIM = 64
V_DIM = 128
Q_RANK = 384
KV_RANK = 256
ROPE_THETA = 10000.0
ATTN_SCALE = (NOPE_DIM + ROPE_DIM) ** -0.5
Q_BLOCK = 128
CHUNK = 128
GMLP_DIM = D_MODEL
GMLP_GROUPS = 8
GROUP_DIM = GMLP_DIM // GMLP_GROUPS
FFN_DIM = 2816

kernel_name = "mla_chunk_gmlp_macaron_adaln_step"


def rms_norm(x, g):
    xf = x.astype(jnp.float32)
    y = xf * lax.rsqrt(jnp.mean(xf * xf, axis=-1, keepdims=True) + EPS)
    return (y * g.astype(jnp.float32)).astype(x.dtype)


def layer_norm(x, g, b):
    xf = x.astype(jnp.float32)
    mu = jnp.mean(xf, axis=-1, keepdims=True)
    var = jnp.mean(jnp.square(xf - mu), axis=-1, keepdims=True)
    y = (xf - mu) * lax.rsqrt(var + EPS)
    return (y * g.astype(jnp.float32) + b.astype(jnp.float32)).astype(x.dtype)


def apply_rope(x, pos):
    half = ROPE_DIM // 2
    freqs = jnp.power(ROPE_THETA, -jnp.arange(half, dtype=jnp.float32) / half)
    ang = (pos[:, None] * freqs[None, :]).reshape((pos.shape[0],) + (1,) * (x.ndim - 3) + (half,))
    cos, sin = jnp.cos(ang), jnp.sin(ang)
    xf = x.astype(jnp.float32)
    x1, x2 = xf[..., :half], xf[..., half:]
    return jnp.concatenate([x1 * cos - x2 * sin, x1 * sin + x2 * cos], axis=-1).astype(x.dtype)


def ada_modulation(c, w, b):
    mod = jax.nn.silu(c) @ w + b
    return mod.reshape(c.shape[0], N_SUB, 3, D_MODEL)


def modulate(x, g, mod, k):
    shift = mod[:, k, 0][:, None]
    scale = mod[:, k, 1][:, None]
    return rms_norm(x, g) * (1 + scale) + shift


def swiglu(h, w_in, w_out):
    gate, up = jnp.split(h @ w_in, 2, axis=-1)
    return (jax.nn.silu(gate) * up) @ w_out


def ffn_sublayer(x, mod, k, g, w_in, w_out):
    y = swiglu(modulate(x, g, mod, k), w_in, w_out)
    return x + HALF_STEP * mod[:, k, 2][:, None] * y


def mla_project(h, pos, w_in, q_norm_g, w_uq, kv_norm_g, w_uk):
    z_q, z_kv, z_kr = jnp.split(h @ w_in, [Q_RANK, Q_RANK + KV_RANK], axis=-1)
    q = jnp.einsum('btr,rhd->bthd', rms_norm(z_q, q_norm_g), w_uq)
    q_nope, q_rope = q[..., :NOPE_DIM], q[..., NOPE_DIM:]
    q_lat = jnp.einsum('bthd,rhd->bthr', q_nope, w_uk)
    latent = rms_norm(z_kv, kv_norm_g)
    return q_lat, apply_rope(q_rope, pos), latent, apply_rope(z_kr, pos)


def latent_attention(q_lat, q_rope, q_pos, segments):
    scores = []
    for lat_k, rope_k, k_pos in segments:
        s = (jnp.einsum('bqhr,bkr->bhqk', q_lat, lat_k)
             + jnp.einsum('bqhd,bkd->bhqk', q_rope, rope_k)).astype(jnp.float32) * ATTN_SCALE
        scores.append(jnp.where(k_pos[None, :] <= q_pos[:, None], s, -jnp.inf))
    p = jax.nn.softmax(jnp.concatenate(scores, axis=-1), axis=-1)
    ctx = None
    off = 0
    for lat_k, _, _ in segments:
        n = lat_k.shape[1]
        part = jnp.einsum('bhqk,bkr->bqhr', p[..., off:off + n].astype(lat_k.dtype), lat_k)
        ctx = part if ctx is None else ctx + part
        off += n
    return ctx


def mla_output(ctx, w_uv, w_o):
    o = jnp.einsum('bthr,rhv->bthv', ctx, w_uv)
    return o.reshape(o.shape[0], o.shape[1], N_HEADS * V_DIM) @ w_o


def mla_prompt(h, w_in, q_norm_g, w_uq, kv_norm_g, w_uk, w_uv, w_o):
    b, t, _ = h.shape
    pos = jnp.arange(t, dtype=jnp.float32)
    q_lat, q_rope, latent, k_rope = mla_project(h, pos, w_in, q_norm_g, w_uq, kv_norm_g, w_uk)
    nqb = t // Q_BLOCK

    def to_blocks(a):
        return a.reshape((b, nqb, Q_BLOCK) + a.shape[2:]).swapaxes(0, 1)

    def block(args):
        ql, qr, qp = args
        return latent_attention(ql, qr, qp, [(latent, k_rope, pos)])

    ctx = lax.map(block, (to_blocks(q_lat), to_blocks(q_rope), pos.reshape(nqb, Q_BLOCK)))
    ctx = ctx.swapaxes(0, 1).reshape(b, t, N_HEADS, KV_RANK)
    return mla_output(ctx, w_uv, w_o), latent, k_rope


def mla_sample(h, lat_pages, rope_pages, w_in, q_norm_g, w_uq, kv_norm_g, w_uk, w_uv, w_o):
    b, t, _ = h.shape
    past = lat_pages.shape[1] * PAGE_SIZE
    pos = jnp.arange(t, dtype=jnp.float32) + past
    q_lat, q_rope, latent, k_rope = mla_project(h, pos, w_in, q_norm_g, w_uq, kv_norm_g, w_uk)
    lat_past = lat_pages.reshape(b, past, KV_RANK)
    rope_past = rope_pages.reshape(b, past, ROPE_DIM)
    past_pos = jnp.arange(past, dtype=jnp.float32)
    ctx = latent_attention(q_lat, q_rope, pos,
                           [(lat_past, rope_past, past_pos), (latent, k_rope, pos)])
    return mla_output(ctx, w_uv, w_o), latent, k_rope


def chunk_gmlp(h, w_in, b_in, ln_g, ln_b, w_s, b_s, w_out):
    b, t, _ = h.shape
    u, v = jnp.split(jax.nn.gelu(h @ w_in + b_in), 2, axis=-1)
    v = layer_norm(v, ln_g, ln_b)
    n_chunks = -(-t // CHUNK)
    pad = n_chunks * CHUNK - t
    vc = jnp.pad(v, ((0, 0), (0, pad), (0, 0))).reshape(b, n_chunks, CHUNK, GMLP_GROUPS, GROUP_DIM)
    causal = jnp.arange(CHUNK)[:, None] >= jnp.arange(CHUNK)[None, :]
    w_causal = jnp.where(causal[None], w_s, 0)
    mixed = jnp.einsum('gts,bcsgd->bctgd', w_causal, vc) + b_s.T[:, :, None]
    mixed = mixed.reshape(b, n_chunks * CHUNK, GMLP_DIM)[:, :t]
    y = (u * mixed) @ w_out
    start = ((t - 1) // CHUNK) * CHUNK
    return y, v[:, start:]


def setup_inputs(seed: int = 0) -> dict:
    key = jax.random.key(seed)
    keys = list(jax.random.split(key, 40))
    f32 = jnp.float32
    n_pages = PAST_LEN // PAGE_SIZE
    n_used = DEC_BATCH * n_pages
    n_pool = n_used + n_used // 4

    def nrm(shape, fan_in, gain=1.0):
        return jax.random.normal(keys.pop(), shape, f32) * (gain * fan_in ** -0.5)

    def gain_vec(shape):
        return 1.0 + 0.02 * jax.random.normal(keys.pop(), shape, f32)

    def small(shape):
        return 0.02 * jax.random.normal(keys.pop(), shape, f32)

    x_prompt = jax.random.normal(keys.pop(), (BATCH, SEQ, D_MODEL), f32)
    x_sample = jax.random.normal(keys.pop(), (DEC_BATCH, DEC_SEQ, D_MODEL), f32)
    cache_kv_latent = jax.random.normal(keys.pop(), (N_A_LAYERS, n_pool, PAGE_SIZE, KV_RANK), f32)
    cache_k_rope = jax.random.normal(keys.pop(), (N_A_LAYERS, n_pool, PAGE_SIZE, ROPE_DIM), f32)
    page_table = jax.random.permutation(keys.pop(), n_pool)[:n_used].reshape(DEC_BATCH, n_pages).astype(jnp.int32)
    c_prompt = jax.random.normal(keys.pop(), (BATCH, D_MODEL), f32)
    c_sample = jax.random.normal(keys.pop(), (DEC_BATCH, D_MODEL), f32)
    return {
        "x_prompt": x_prompt,
        "x_sample": x_sample,
        "cache_kv_latent": cache_kv_latent,
        "cache_k_rope": cache_k_rope,
        "page_table": page_table,
        "c_prompt": c_prompt,
        "c_sample": c_sample,
        "ada_w": nrm((DEPTH, D_MODEL, N_SUB * 3 * D_MODEL), D_MODEL, 0.3),
        "ada_b": small((DEPTH, N_SUB * 3 * D_MODEL)),
        "norm_g": gain_vec((DEPTH, N_SUB, D_MODEL)),
        "ffn_w_in": nrm((DEPTH, 2, D_MODEL, 2 * FFN_DIM), D_MODEL),
        "ffn_w_out": nrm((DEPTH, 2, FFN_DIM, D_MODEL), FFN_DIM),
        "a_w_in": nrm((N_A_LAYERS, D_MODEL, Q_RANK + KV_RANK + ROPE_DIM), D_MODEL),
        "a_q_norm": gain_vec((N_A_LAYERS, Q_RANK)),
        "a_w_uq": nrm((N_A_LAYERS, Q_RANK, N_HEADS, NOPE_DIM + ROPE_DIM), Q_RANK),
        "a_kv_norm": gain_vec((N_A_LAYERS, KV_RANK)),
        "a_w_uk": nrm((N_A_LAYERS, KV_RANK, N_HEADS, NOPE_DIM), KV_RANK),
        "a_w_uv": nrm((N_A_LAYERS, KV_RANK, N_HEADS, V_DIM), KV_RANK),
        "a_w_o": nrm((N_A_LAYERS, N_HEADS * V_DIM, D_MODEL), N_HEADS * V_DIM),
        "b_w_in": nrm((N_B_LAYERS, D_MODEL, 2 * GMLP_DIM), D_MODEL),
        "b_b_in": small((N_B_LAYERS, 2 * GMLP_DIM)),
        "b_ln_g": gain_vec((N_B_LAYERS, GMLP_DIM)),
        "b_ln_b": small((N_B_LAYERS, GMLP_DIM)),
        "b_w_s": nrm((N_B_LAYERS, GMLP_GROUPS, CHUNK, CHUNK), CHUNK),
        "b_b_s": gain_vec((N_B_LAYERS, GMLP_GROUPS, CHUNK)),
        "b_w_out": nrm((N_B_LAYERS, GMLP_DIM, D_MODEL), GMLP_DIM),
        "final_g": gain_vec((D_MODEL,)),
    }


def reference(x_prompt, x_sample, cache_kv_latent, cache_k_rope, page_table, c_prompt, c_sample,
              ada_w, ada_b, norm_g, ffn_w_in, ffn_w_out,
              a_w_in, a_q_norm, a_w_uq, a_kv_norm, a_w_uk, a_w_uv, a_w_o,
              b_w_in, b_b_in, b_ln_g, b_ln_b, b_w_s, b_b_s, b_w_out, final_g):
    xp, xs = x_prompt, x_sample
    lat_p, rope_p, lat_s, rope_s, v_p, v_s = [], [], [], [], [], []
    for i in range(DEPTH):
        mp = ada_modulation(c_prompt, ada_w[i], ada_b[i])
        ms = ada_modulation(c_sample, ada_w[i], ada_b[i])
        xp = ffn_sublayer(xp, mp, 0, norm_g[i, 0], ffn_w_in[i, 0], ffn_w_out[i, 0])
        xs = ffn_sublayer(xs, ms, 0, norm_g[i, 0], ffn_w_in[i, 0], ffn_w_out[i, 0])
        hp = modulate(xp, norm_g[i, 1], mp, 1)
        hs = modulate(xs, norm_g[i, 1], ms, 1)
        j = i // N_MIXERS
        if i % N_MIXERS == 0:
            wa = (a_w_in[j], a_q_norm[j], a_w_uq[j], a_kv_norm[j], a_w_uk[j], a_w_uv[j], a_w_o[j])
            yp, lp, rp = mla_prompt(hp, *wa)
            ys, ls, rs = mla_sample(hs, cache_kv_latent[j, page_table], cache_k_rope[j, page_table], *wa)
            lat_p.append(lp)
            rope_p.append(rp)
            lat_s.append(ls)
            rope_s.append(rs)
        else:
            wb = (b_w_in[j], b_b_in[j], b_ln_g[j], b_ln_b[j], b_w_s[j], b_b_s[j], b_w_out[j])
            yp, vp = chunk_gmlp(hp, *wb)
            ys, vs = chunk_gmlp(hs, *wb)
            v_p.append(vp)
            v_s.append(vs)
        xp = xp + mp[:, 1, 2][:, None] * yp
        xs = xs + ms[:, 1, 2][:, None] * ys
        xp = ffn_sublayer(xp, mp, 2, norm_g[i, 2], ffn_w_in[i, 1], ffn_w_out[i, 1])
        xs = ffn_sublayer(xs, ms, 2, norm_g[i, 2], ffn_w_in[i, 1], ffn_w_out[i, 1])
    y_prompt = rms_norm(xp, final_g)
    y_sample = rms_norm(xs, final_g)
    return (y_prompt, y_sample, jnp.stack(lat_p), jnp.stack(rope_p), jnp.stack(lat_s),
            jnp.stack(rope_s), jnp.stack(v_p), jnp.stack(v_s))
```

```python
import functools

import jax
import jax.numpy as jnp
import numpy as np
from jax import lax
from jax.experimental import pallas as pl
from jax.experimental.pallas import tpu as pltpu

D_MODEL = 1024
BATCH = 8
SEQ = 2048
DEPTH = 4
DEC_BATCH = 128
DEC_SEQ = 8
PAST_LEN = 8192
PAGE_SIZE = 128
N_PAGES = PAST_LEN // PAGE_SIZE
N_SUB = 3
HALF_STEP = 0.5
EPS = 1e-6
N_HEADS = 8
NOPE_DIM = 128
ROPE_DIM = 64
V_DIM = 128
Q_RANK = 384
KV_RANK = 256
ROPE_THETA = 10000.0
ATTN_SCALE = (NOPE_DIM + ROPE_DIM) ** -0.5
CHUNK = 128
GMLP_DIM = D_MODEL
GMLP_GROUPS = 8
GROUP_DIM = GMLP_DIM // GMLP_GROUPS
FFN_DIM = 2816

P_ROWS = BATCH * SEQ
S_ROWS = DEC_BATCH * DEC_SEQ
ROWS = P_ROWS + S_ROWS
TM = 512
N_TILES = ROWS // TM
N_PTILES = P_ROWS // TM
TILES_PER_SEQ = SEQ // TM
FFN_CHUNK = 256
N_FFN_CHUNKS = FFN_DIM // FFN_CHUNK
QK_DIM = KV_RANK + ROPE_DIM
Q_HEAD_COLS = 2 * NOPE_DIM
TQ = 128
TK = 256
PAGES_PER_STEP = 16
N_PAGE_STEPS = N_PAGES // PAGES_PER_STEP
MOD_TN = 1152
VMEM_LIMIT = 56 * 1024 * 1024

F32 = jnp.float32
BF16 = jnp.bfloat16


def _params(sem):
    return pltpu.CompilerParams(dimension_semantics=sem, vmem_limit_bytes=VMEM_LIMIT)


def _resident(shape):
    nd = len(shape)
    return pl.BlockSpec(shape, lambda *_: (0,) * nd, pipeline_mode=pl.Buffered(1))


def _mod_specs():
    modp = pl.BlockSpec((1, 3, D_MODEL),
                        lambda i: (jnp.minimum(i // TILES_PER_SEQ, BATCH - 1), 0, 0))
    mods = pl.BlockSpec((3, TM, D_MODEL), lambda i: (0, jnp.maximum(i - N_PTILES, 0), 0))
    return [modp, mods]


def _mod_vec(modp_ref, mods_ref, j):
    is_p = pl.program_id(0) < N_PTILES
    return jnp.where(is_p, modp_ref[0, j:j + 1, :], mods_ref[j])


def _modulated(x, g_ref, modp_ref, mods_ref):
    shift = _mod_vec(modp_ref, mods_ref, 0)
    scale = _mod_vec(modp_ref, mods_ref, 1)
    y = x * lax.rsqrt(jnp.mean(x * x, axis=-1, keepdims=True) + EPS) * g_ref[...]
    return (y * (1 + scale) + shift).astype(BF16)


def _dot(a, b):
    return jnp.dot(a, b, preferred_element_type=F32)


def _dot_nt(a, b):
    return lax.dot_general(a, b, (((1,), (1,)), ((), ())), preferred_element_type=F32)


def _mod_kernel(c_ref, w_ref, b_ref, o_ref):
    c = c_ref[...]
    a = (c * jax.nn.sigmoid(c)).astype(BF16)
    o_ref[0] = _dot(a, w_ref[0].astype(BF16)) + b_ref[0]


def _ada_modulation(c_all, ada_w, ada_b):
    n = c_all.shape[0]
    width = N_SUB * 3 * D_MODEL
    return pl.pallas_call(
        _mod_kernel,
        grid=(DEPTH, width // MOD_TN),
        in_specs=[
            pl.BlockSpec((n, D_MODEL), lambda l, j: (0, 0)),
            pl.BlockSpec((1, D_MODEL, MOD_TN), lambda l, j: (l, 0, j)),
            pl.BlockSpec((1, 1, MOD_TN), lambda l, j: (l, 0, j)),
        ],
        out_specs=pl.BlockSpec((1, n, MOD_TN), lambda l, j: (l, 0, j)),
        out_shape=jax.ShapeDtypeStruct((DEPTH, n, width), F32),
        compiler_params=_params(("arbitrary", "arbitrary")),
        name="ada_modulation",
    )(c_all, ada_w, ada_b.reshape(DEPTH, 1, width))


def _ffn_kernel(x_ref, modp_ref, mods_ref, g_ref, wg_ref, wu_ref, wo_ref, o_ref, h_s, acc_s):
    x = x_ref[...]
    h_s[...] = _modulated(x, g_ref, modp_ref, mods_ref)
    acc_s[...] = jnp.zeros_like(acc_s)

    def body(c, carry):
        h = h_s[...]
        gate = _dot(h, wg_ref[c])
        up = _dot(h, wu_ref[c])
        a = (gate * jax.nn.sigmoid(gate) * up).astype(BF16)
        acc_s[...] += _dot(a, wo_ref[c])
        return carry

    lax.fori_loop(0, N_FFN_CHUNKS, body, 0)
    o_ref[...] = x + (HALF_STEP * _mod_vec(modp_ref, mods_ref, 2)) * acc_s[...]


def _ffn_sublayer(x, modp, mods, g, wg, wu, wo):
    row = pl.BlockSpec((TM, D_MODEL), lambda i: (i, 0))
    return pl.pallas_call(
        _ffn_kernel,
        grid=(N_TILES,),
        in_specs=[row] + _mod_specs() + [
            _resident((1, D_MODEL)),
            _resident((N_FFN_CHUNKS, D_MODEL, FFN_CHUNK)),
            _resident((N_FFN_CHUNKS, D_MODEL, FFN_CHUNK)),
            _resident((N_FFN_CHUNKS, FFN_CHUNK, D_MODEL)),
        ],
        out_specs=row,
        out_shape=jax.ShapeDtypeStruct((ROWS, D_MODEL), F32),
        scratch_shapes=[pltpu.VMEM((TM, D_MODEL), BF16), pltpu.VMEM((TM, D_MODEL), F32)],
        compiler_params=_params(("arbitrary",)),
        name="ffn_sublayer",
    )(x, modp, mods, g, wg, wu, wo)


def _rope(t, table):
    w = t * table
    return w + pltpu.roll(w, ROPE_DIM, axis=1)


def _mla_proj_kernel(x_ref, modp_ref, mods_ref, g_ref, win_ref, qn_ref, wuq_ref, kvn_ref,
                     wuk_ref, tab_ref, q_ref, kv_ref, lat_ref, kr_ref):
    h = _modulated(x_ref[...], g_ref, modp_ref, mods_ref)
    z = _dot(h, win_ref[...])
    z_q = z[:, :Q_RANK]
    z_kv = z[:, Q_RANK:Q_RANK + KV_RANK]
    table = tab_ref[...]
    k_rope = _rope(z[:, Q_RANK + KV_RANK:], table)[:, :ROPE_DIM]
    latent = (z_kv * lax.rsqrt(jnp.mean(z_kv * z_kv, axis=-1, keepdims=True) + EPS)
              * kvn_ref[...])
    lat_ref[...] = latent
    kr_ref[...] = k_rope
    kv_ref[:, :KV_RANK] = latent.astype(BF16)
    kv_ref[:, KV_RANK:] = k_rope.astype(BF16)
    qn = (z_q * lax.rsqrt(jnp.mean(z_q * z_q, axis=-1, keepdims=True) + EPS)
          * qn_ref[...]).astype(BF16)
    q_all = _dot(qn, wuq_ref[...])
    for hd in range(N_HEADS):
        base = hd * Q_HEAD_COLS
        q_nope = q_all[:, base:base + NOPE_DIM].astype(BF16)
        q_lat = _dot(q_nope, wuk_ref[hd]) * ATTN_SCALE
        q_rope = _rope(q_all[:, base + NOPE_DIM:base + Q_HEAD_COLS], table)[:, :ROPE_DIM]
        q_ref[hd, :, :KV_RANK] = q_lat.astype(BF16)
        q_ref[hd, :, KV_RANK:] = (q_rope * ATTN_SCALE).astype(BF16)


def _mla_project(x, modp, mods, g, w_in_ext, q_norm, w_uq_ext, kv_norm, w_uk_t, table):
    row = pl.BlockSpec((TM, D_MODEL), lambda i: (i, 0))
    return pl.pallas_call(
        _mla_proj_kernel,
        grid=(N_TILES,),
        in_specs=[row] + _mod_specs() + [
            _resident((1, D_MODEL)),
            _resident(w_in_ext.shape),
            _resident((1, Q_RANK)),
            _resident(w_uq_ext.shape),
            _resident((1, KV_RANK)),
            _resident(w_uk_t.shape),
            pl.BlockSpec((TM, 2 * ROPE_DIM), lambda i: (i, 0)),
        ],
        out_specs=[
            pl.BlockSpec((N_HEADS, TM, QK_DIM), lambda i: (0, i, 0)),
            pl.BlockSpec((TM, QK_DIM), lambda i: (i, 0)),
            pl.BlockSpec((TM, KV_RANK), lambda i: (i, 0)),
            pl.BlockSpec((TM, ROPE_DIM), lambda i: (i, 0)),
        ],
        out_shape=[
            jax.ShapeDtypeStruct((N_HEADS, ROWS, QK_DIM), BF16),
            jax.ShapeDtypeStruct((ROWS, QK_DIM), BF16),
            jax.ShapeDtypeStruct((ROWS, KV_RANK), F32),
            jax.ShapeDtypeStruct((ROWS, ROPE_DIM), F32),
        ],
        compiler_params=_params(("arbitrary",)),
        name="mla_project",
    )(x, modp, mods, g, w_in_ext, q_norm, w_uq_ext, kv_norm, w_uk_t, table)


def _attn_prompt_kernel(q_ref, kv_ref, o_ref, m_s, l_s, acc_s):
    qi = pl.program_id(1)
    rows = N_HEADS * TQ
    q = q_ref[...].reshape(rows, QK_DIM)
    m_s[...] = jnp.full_like(m_s, -jnp.inf)
    l_s[...] = jnp.zeros_like(l_s)
    acc_s[...] = jnp.zeros_like(acc_s)

    def step(kb, masked):
        k = kv_ref[pl.ds(pl.multiple_of(kb * TK, TK), TK), :]
        s = _dot_nt(q, k)
        if masked:
            q_pos = qi * TQ + (lax.broadcasted_iota(jnp.int32, (rows, TK), 0) & (TQ - 1))
            k_pos = kb * TK + lax.broadcasted_iota(jnp.int32, (rows, TK), 1)
            s = jnp.where(k_pos <= q_pos, s, -jnp.inf)
        m_old = m_s[...]
        m_new = jnp.maximum(m_old, jnp.max(s, axis=-1, keepdims=True))
        alpha = jnp.exp(m_old - m_new)
        p = jnp.exp(s - m_new)
        l_s[...] = alpha * l_s[...] + jnp.sum(p, axis=-1, keepdims=True)
        acc_s[...] = alpha * acc_s[...] + _dot(p.astype(BF16), k[:, :KV_RANK])
        m_s[...] = m_new

    n_full = (qi * TQ) // TK

    def body(kb, carry):
        step(kb, False)
        return carry

    lax.fori_loop(0, n_full, body, 0)
    step(n_full, True)
    ctx = acc_s[...] / l_s[...]
    for hd in range(N_HEADS):
        o_ref[:, hd * KV_RANK:(hd + 1) * KV_RANK] = ctx[hd * TQ:(hd + 1) * TQ].astype(BF16)


def _attn_prompt(q, kv):
    nq = SEQ // TQ
    rows = N_HEADS * TQ
    return pl.pallas_call(
        _attn_prompt_kernel,
        grid=(BATCH, nq),
        in_specs=[
            pl.BlockSpec((N_HEADS, TQ, QK_DIM), lambda b, i: (0, b * nq + i, 0)),
            pl.BlockSpec((SEQ, QK_DIM), lambda b, i: (b, 0)),
        ],
        out_specs=pl.BlockSpec((TQ, N_HEADS * KV_RANK), lambda b, i: (b * nq + i, 0)),
        out_shape=jax.ShapeDtypeStruct((P_ROWS, N_HEADS * KV_RANK), BF16),
        scratch_shapes=[pltpu.VMEM((rows, 1), F32), pltpu.VMEM((rows, 1), F32),
                        pltpu.VMEM((rows, KV_RANK), F32)],
        compiler_params=_params(("arbitrary", "arbitrary")),
        name="attn_prompt",
    )(q, kv)


def _attn_sample_kernel(pt_ref, q_ref, ln_ref, rn_ref, *refs):
    lat_pages = refs[:PAGES_PER_STEP]
    rope_pages = refs[PAGES_PER_STEP:2 * PAGES_PER_STEP]
    o_ref, m_s, l_s, acc_s, kl_s, kr_s = refs[2 * PAGES_PER_STEP:]
    step = pl.program_id(1)
    rows = N_HEADS * DEC_SEQ

    @pl.when(step == 0)
    def _():
        m_s[...] = jnp.full_like(m_s, -jnp.inf)
        l_s[...] = jnp.zeros_like(l_s)
        acc_s[...] = jnp.zeros_like(acc_s)

    q = q_ref[0]
    q_lat = q[:, :KV_RANK]
    q_rope = q[:, KV_RANK:]

    def update(s, v):
        m_old = m_s[...]
        m_new = jnp.maximum(m_old, jnp.max(s, axis=-1, keepdims=True))
        alpha = jnp.exp(m_old - m_new)
        p = jnp.exp(s - m_new)
        l_s[...] = alpha * l_s[...] + jnp.sum(p, axis=-1, keepdims=True)
        acc_s[...] = alpha * acc_s[...] + _dot(p.astype(BF16), v)
        m_s[...] = m_new

    for p in range(PAGES_PER_STEP):
        kl_s[p * PAGE_SIZE:(p + 1) * PAGE_SIZE, :] = lat_pages[p][0, 0].astype(BF16)
        kr_s[p * PAGE_SIZE:(p + 1) * PAGE_SIZE, :] = rope_pages[p][0, 0].astype(BF16)
    k_lat = kl_s[...]
    update(_dot_nt(q_lat, k_lat) + _dot_nt(q_rope, kr_s[...]), k_lat)

    @pl.when(step == N_PAGE_STEPS - 1)
    def _():
        new_lat = ln_ref[...].astype(BF16)
        new_rope = rn_ref[...].astype(BF16)
        s = _dot_nt(q_lat, new_lat) + _dot_nt(q_rope, new_rope)
        q_pos = lax.broadcasted_iota(jnp.int32, (rows, DEC_SEQ), 0) & (DEC_SEQ - 1)
        k_pos = lax.broadcasted_iota(jnp.int32, (rows, DEC_SEQ), 1)
        update(jnp.where(k_pos <= q_pos, s, -jnp.inf), new_lat)
        ctx = acc_s[...] / l_s[...]
        for hd in range(N_HEADS):
            o_ref[0, :, hd * KV_RANK:(hd + 1) * KV_RANK] = ctx[hd * DEC_SEQ:(hd + 1) * DEC_SEQ]


def _attn_sample(layer, page_table, q_s, lat, k_rope, cache_lat, cache_rope):
    rows = N_HEADS * DEC_SEQ
    new_row0 = P_ROWS // DEC_SEQ

    def page_spec(width, p):
        return pl.BlockSpec(
            (1, 1, PAGE_SIZE, width),
            lambda b, s, pt: (layer, pt[b, s * PAGES_PER_STEP + p], 0, 0))

    in_specs = [
        pl.BlockSpec((1, rows, QK_DIM), lambda b, s, pt: (b, 0, 0)),
        pl.BlockSpec((DEC_SEQ, KV_RANK), lambda b, s, pt: (new_row0 + b, 0)),
        pl.BlockSpec((DEC_SEQ, ROPE_DIM), lambda b, s, pt: (new_row0 + b, 0)),
    ]
    in_specs += [page_spec(KV_RANK, p) for p in range(PAGES_PER_STEP)]
    in_specs += [page_spec(ROPE_DIM, p) for p in range(PAGES_PER_STEP)]
    keys = PAGES_PER_STEP * PAGE_SIZE
    return pl.pallas_call(
        _attn_sample_kernel,
        grid_spec=pltpu.PrefetchScalarGridSpec(
            num_scalar_prefetch=1,
            grid=(DEC_BATCH, N_PAGE_STEPS),
            in_specs=in_specs,
            out_specs=pl.BlockSpec((1, DEC_SEQ, N_HEADS * KV_RANK), lambda b, s, pt: (b, 0, 0)),
            scratch_shapes=[pltpu.VMEM((rows, 1), F32), pltpu.VMEM((rows, 1), F32),
                            pltpu.VMEM((rows, KV_RANK), F32),
                            pltpu.VMEM((keys, KV_RANK), BF16), pltpu.VMEM((keys, ROPE_DIM), BF16)],
        ),
        out_shape=jax.ShapeDtypeStruct((DEC_BATCH, DEC_SEQ, N_HEADS * KV_RANK), F32),
        compiler_params=_params(("arbitrary", "arbitrary")),
        name="attn_sample",
    )(page_table, q_s, lat, k_rope, *([cache_lat] * PAGES_PER_STEP), *([cache_rope] * PAGES_PER_STEP))


def _mla_out_kernel(x_ref, modp_ref, mods_ref, cp_ref, cs_ref, wuv_ref, wo_ref, o_ref, c_s, o_s):
    is_p = pl.program_id(0) < N_PTILES

    @pl.when(is_p)
    def _():
        c_s[...] = cp_ref[...]

    @pl.when(jnp.logical_not(is_p))
    def _():
        c_s[...] = cs_ref[...].astype(BF16)

    for hd in range(N_HEADS):
        o_h = _dot(c_s[:, hd * KV_RANK:(hd + 1) * KV_RANK], wuv_ref[hd])
        o_s[:, hd * V_DIM:(hd + 1) * V_DIM] = o_h.astype(BF16)
    y = _dot(o_s[...], wo_ref[...])
    o_ref[...] = x_ref[...] + _mod_vec(modp_ref, mods_ref, 2) * y


def _mla_output(x, modp, mods, ctx_p, ctx_s, w_uv_h, w_o):
    row = pl.BlockSpec((TM, D_MODEL), lambda i: (i, 0))
    width = N_HEADS * KV_RANK
    return pl.pallas_call(
        _mla_out_kernel,
        grid=(N_TILES,),
        in_specs=[row] + _mod_specs() + [
            pl.BlockSpec((TM, width), lambda i: (jnp.minimum(i, N_PTILES - 1), 0)),
            pl.BlockSpec((TM, width), lambda i: (jnp.maximum(i - N_PTILES, 0), 0)),
            _resident(w_uv_h.shape),
            _resident(w_o.shape),
        ],
        out_specs=row,
        out_shape=jax.ShapeDtypeStruct((ROWS, D_MODEL), F32),
        scratch_shapes=[pltpu.VMEM((TM, width), BF16), pltpu.VMEM((TM, N_HEADS * V_DIM), BF16)],
        compiler_params=_params(("arbitrary",)),
        name="mla_output",
    )(x, modp, mods, ctx_p, ctx_s, w_uv_h, w_o)


def _gmlp_kernel(x_ref, modp_ref, mods_ref, g_ref, win_ref, bin_ref, lng_ref, lnb_ref,
                 wmp_ref, wms_ref, bp_ref, bs_ref, wout_ref, o_ref, v_ref, vb_s, t_s):
    is_p = pl.program_id(0) < N_PTILES
    x = x_ref[...]
    h = _modulated(x, g_ref, modp_ref, mods_ref)
    z = jax.nn.gelu(_dot(h, win_ref[...]) + bin_ref[...])
    v = z[:, GMLP_DIM:]
    mu = jnp.mean(v, axis=-1, keepdims=True)
    var = jnp.mean(jnp.square(v - mu), axis=-1, keepdims=True)
    vn = (v - mu) * lax.rsqrt(var + EPS) * lng_ref[...] + lnb_ref[...]
    v_ref[...] = vn
    vb_s[...] = vn.astype(BF16)
    r = lax.broadcasted_iota(jnp.int32, (CHUNK, CHUNK), 0)
    c = lax.broadcasted_iota(jnp.int32, (CHUNK, CHUNK), 1)
    same_seq = jnp.logical_or(is_p, jnp.bitwise_xor(r, c) < DEC_SEQ)
    keep = jnp.logical_and(r >= c, same_seq)
    bias = jnp.where(is_p, bp_ref[...], bs_ref[...])
    for g in range(GMLP_GROUPS):
        cols = slice(g * GROUP_DIM, (g + 1) * GROUP_DIM)
        w = jnp.where(keep, jnp.where(is_p, wmp_ref[g], wms_ref[g]), 0.0).astype(BF16)
        for ch in range(TM // CHUNK):
            rws = slice(ch * CHUNK, (ch + 1) * CHUNK)
            mixed = _dot(w, vb_s[rws, cols]) + bias[:, cols]
            t_s[rws, cols] = (z[rws, cols] * mixed).astype(BF16)
    y = _dot(t_s[...], wout_ref[...])
    o_ref[...] = x + _mod_vec(modp_ref, mods_ref, 2) * y


def _gmlp_sublayer(x, modp, mods, g, w_in, b_in, ln_g, ln_b, wm_p, wm_s, bias_p, bias_s, w_out):
    row = pl.BlockSpec((TM, D_MODEL), lambda i: (i, 0))
    return pl.pallas_call(
        _gmlp_kernel,
        grid=(N_TILES,),
        in_specs=[row] + _mod_specs() + [
            _resident((1, D_MODEL)),
            _resident(w_in.shape),
            _resident((1, 2 * GMLP_DIM)),
            _resident((1, GMLP_DIM)),
            _resident((1, GMLP_DIM)),
            _resident(wm_p.shape),
            _resident(wm_s.shape),
            _resident(bias_p.shape),
            _resident(bias_s.shape),
            _resident(w_out.shape),
        ],
        out_specs=[row, pl.BlockSpec((TM, GMLP_DIM), lambda i: (i, 0))],
        out_shape=[jax.ShapeDtypeStruct((ROWS, D_MODEL), F32),
                   jax.ShapeDtypeStruct((ROWS, GMLP_DIM), F32)],
        scratch_shapes=[pltpu.VMEM((TM, GMLP_DIM), BF16), pltpu.VMEM((TM, GMLP_DIM), BF16)],
        compiler_params=_params(("arbitrary",)),
        name="gmlp_sublayer",
    )(x, modp, mods, g, w_in, b_in, ln_g, ln_b, wm_p, wm_s, bias_p, bias_s, w_out)


def _final_kernel(x_ref, g_ref, o_ref):
    x = x_ref[...]
    o_ref[...] = x * lax.rsqrt(jnp.mean(x * x, axis=-1, keepdims=True) + EPS) * g_ref[...]


def _final_norm(x, g):
    row = pl.BlockSpec((TM, D_MODEL), lambda i: (i, 0))
    return pl.pallas_call(
        _final_kernel,
        grid=(N_TILES,),
        in_specs=[row, _resident((1, D_MODEL))],
        out_specs=row,
        out_shape=jax.ShapeDtypeStruct((ROWS, D_MODEL), F32),
        compiler_params=_params(("arbitrary",)),
        name="final_norm",
    )(x, g)


def _rope_table():
    half = ROPE_DIM // 2
    freqs = jnp.power(ROPE_THETA, -jnp.arange(half, dtype=F32) / half)
    pos_p = jnp.tile(jnp.arange(SEQ, dtype=F32), BATCH)
    pos_s = jnp.tile(jnp.arange(DEC_SEQ, dtype=F32) + PAST_LEN, DEC_BATCH)
    ang = jnp.concatenate([pos_p, pos_s])[:, None] * freqs[None, :]
    cos, sin = jnp.cos(ang), jnp.sin(ang)
    return jnp.concatenate([cos, cos, -sin, sin], axis=-1)


def _with_rotated(w):
    half = ROPE_DIM // 2
    return jnp.concatenate([w, w[..., -half:], w[..., -ROPE_DIM:-half]], axis=-1)


def _split_mods(mod_l):
    out = []
    for k in range(N_SUB):
        modp = mod_l[:BATCH, k]
        mods = jnp.repeat(mod_l[BATCH:, k], DEC_SEQ, axis=0).transpose(1, 0, 2)
        out.append((modp, mods))
    return out


def kernel(x_prompt, x_sample, cache_kv_latent, cache_k_rope, page_table, c_prompt, c_sample,
           ada_w, ada_b, norm_g, ffn_w_in, ffn_w_out,
           a_w_in, a_q_norm, a_w_uq, a_kv_norm, a_w_uk, a_w_uv, a_w_o,
           b_w_in, b_b_in, b_ln_g, b_ln_b, b_w_s, b_b_s, b_w_out, final_g):
    x = jnp.concatenate([x_prompt.reshape(P_ROWS, D_MODEL), x_sample.reshape(S_ROWS, D_MODEL)])
    c_all = jnp.concatenate([c_prompt, c_sample])
    mod = _ada_modulation(c_all, ada_w, ada_b).reshape(DEPTH, BATCH + DEC_BATCH, N_SUB, 3, D_MODEL)
    table = _rope_table()

    w_in = ffn_w_in.astype(BF16).reshape(DEPTH, 2, D_MODEL, 2, N_FFN_CHUNKS, FFN_CHUNK)
    w_in = w_in.transpose(0, 1, 3, 4, 2, 5)
    w_out = ffn_w_out.astype(BF16).reshape(DEPTH, 2, N_FFN_CHUNKS, FFN_CHUNK, D_MODEL)

    lat_all, rope_all, v_all = [], [], []
    for i in range(DEPTH):
        mods_i = _split_mods(mod[i])
        j = i // 2
        x = _ffn_sublayer(x, *mods_i[0], norm_g[i, 0][None], w_in[i, 0, 0], w_in[i, 0, 1], w_out[i, 0])
        if i % 2 == 0:
            w_in_ext = _with_rotated(a_w_in[j]).astype(BF16)
            w_uq_ext = _with_rotated(a_w_uq[j]).reshape(Q_RANK, N_HEADS * Q_HEAD_COLS).astype(BF16)
            w_uk_t = a_w_uk[j].transpose(1, 2, 0).astype(BF16)
            w_uv_h = a_w_uv[j].transpose(1, 0, 2).astype(BF16)
            q, kv, lat, k_rope = _mla_project(
                x, *mods_i[1], norm_g[i, 1][None], w_in_ext, a_q_norm[j][None], w_uq_ext,
                a_kv_norm[j][None], w_uk_t, table)
            ctx_p = _attn_prompt(q, kv)
            q_s = q[:, P_ROWS:].reshape(N_HEADS, DEC_BATCH, DEC_SEQ, QK_DIM)
            q_s = q_s.transpose(1, 0, 2, 3).reshape(DEC_BATCH, N_HEADS * DEC_SEQ, QK_DIM)
            ctx_s = _attn_sample(j, page_table, q_s, lat, k_rope, cache_kv_latent, cache_k_rope)
            x = _mla_output(x, *mods_i[1], ctx_p, ctx_s.reshape(S_ROWS, N_HEADS * KV_RANK),
                            w_uv_h, a_w_o[j].astype(BF16))
            lat_all.append(lat)
            rope_all.append(k_rope)
        else:
            wm_s = jnp.tile(b_w_s[j][:, :DEC_SEQ, :DEC_SEQ], (1, CHUNK // DEC_SEQ, CHUNK // DEC_SEQ))
            bias_p = jnp.repeat(b_b_s[j].T, GROUP_DIM, axis=1)
            bias_s = jnp.tile(bias_p[:DEC_SEQ], (CHUNK // DEC_SEQ, 1))
            x, v = _gmlp_sublayer(
                x, *mods_i[1], norm_g[i, 1][None], b_w_in[j].astype(BF16), b_b_in[j][None],
                b_ln_g[j][None], b_ln_b[j][None], b_w_s[j], wm_s, bias_p, bias_s,
                b_w_out[j].astype(BF16))
            v_all.append(v)
        x = _ffn_sublayer(x, *mods_i[2], norm_g[i, 2][None], w_in[i, 1, 0], w_in[i, 1, 1], w_out[i, 1])

    y = _final_norm(x, final_g[None])

    def prompt_part(a, width):
        return a[:P_ROWS].reshape(BATCH, SEQ, width)

    def sample_part(a, width):
        return a[P_ROWS:].reshape(DEC_BATCH, DEC_SEQ, width)

    return (
        prompt_part(y, D_MODEL),
        sample_part(y, D_MODEL),
        jnp.stack([prompt_part(a, KV_RANK) for a in lat_all]),
        jnp.stack([prompt_part(a, ROPE_DIM) for a in rope_all]),
        jnp.stack([sample_part(a, KV_RANK) for a in lat_all]),
        jnp.stack([sample_part(a, ROPE_DIM) for a in rope_all]),
        jnp.stack([prompt_part(a, GMLP_DIM)[:, SEQ - CHUNK:] for a in v_all]),
        jnp.stack([sample_part(a, GMLP_DIM) for a in v_all]),
    )
```

```python
import jax
import jax.numpy as jnp
from jax import lax
from jax.experimental import pallas as pl
from jax.experimental.pallas import tpu as pltpu

D_MODEL = 1024
BATCH = 8
SEQ = 2048
DEPTH = 4
DEC_BATCH = 128
DEC_SEQ = 8
PAST_LEN = 8192
PAGE_SIZE = 128
N_PAGES = PAST_LEN // PAGE_SIZE
N_SUB = 3
HALF_STEP = 0.5
EPS = 1e-6
N_HEADS = 8
NOPE_DIM = 128
ROPE_DIM = 64
V_DIM = 128
Q_RANK = 384
KV_RANK = 256
ROPE_THETA = 10000.0
ATTN_SCALE = (NOPE_DIM + ROPE_DIM) ** -0.5
CHUNK = 128
GMLP_DIM = D_MODEL
GMLP_GROUPS = 8
GROUP_DIM = GMLP_DIM // GMLP_GROUPS
FFN_DIM = 2816

P_ROWS = BATCH * SEQ
S_ROWS = DEC_BATCH * DEC_SEQ
ROWS = P_ROWS + S_ROWS
TM = 512
N_TILES = ROWS // TM
N_PTILES = P_ROWS // TM
TILES_PER_SEQ = SEQ // TM
FFN_CHUNK = 256
N_FFN_CHUNKS = FFN_DIM // FFN_CHUNK
QK_DIM = KV_RANK + ROPE_DIM
HEAD_DIM = NOPE_DIM + ROPE_DIM
Q_HEAD_COLS = 2 * NOPE_DIM
TQ = 256
TK = 256
N_KBLOCKS = SEQ // TK
PAGES_PER_STEP = 16
N_PAGE_STEPS = N_PAGES // PAGES_PER_STEP
MOD_TN = 1152
VMEM_LIMIT = 56 * 1024 * 1024

F32 = jnp.float32
BF16 = jnp.bfloat16


def _params(sem):
    return pltpu.CompilerParams(dimension_semantics=sem, vmem_limit_bytes=VMEM_LIMIT)


def _resident(shape):
    nd = len(shape)
    return pl.BlockSpec(shape, lambda *_: (0,) * nd, pipeline_mode=pl.Buffered(1))


def _ptile(i):
    return jnp.minimum(i, N_PTILES - 1)


def _stile(i):
    return jnp.maximum(i - N_PTILES, 0)


def _mod_specs():
    modp = pl.BlockSpec((1, 3, D_MODEL),
                        lambda i: (jnp.minimum(i // TILES_PER_SEQ, BATCH - 1), 0, 0))
    mods = pl.BlockSpec((3, TM, D_MODEL), lambda i: (0, _stile(i), 0))
    return [modp, mods]


def _mod_vec(modp_ref, mods_ref, j):
    is_p = pl.program_id(0) < N_PTILES
    return jnp.where(is_p, modp_ref[0, j:j + 1, :], mods_ref[j])


def _rms(x):
    return x * lax.rsqrt(jnp.mean(x * x, axis=-1, keepdims=True) + EPS)


def _modulated(x, g_ref, modp_ref, mods_ref):
    shift = _mod_vec(modp_ref, mods_ref, 0)
    scale = _mod_vec(modp_ref, mods_ref, 1)
    return (_rms(x) * g_ref[...] * (1 + scale) + shift).astype(BF16)


def _dot(a, b):
    return jnp.dot(a, b, preferred_element_type=F32)


def _dot_nt(a, b):
    return lax.dot_general(a, b, (((1,), (1,)), ((), ())), preferred_element_type=F32)


def _mod_kernel(c_ref, w_ref, b_ref, o_ref):
    c = c_ref[...]
    a = (c * jax.nn.sigmoid(c)).astype(BF16)
    o_ref[0] = _dot(a, w_ref[0].astype(BF16)) + b_ref[0]


def _ada_modulation(c_all, ada_w, ada_b):
    n = c_all.shape[0]
    width = N_SUB * 3 * D_MODEL
    return pl.pallas_call(
        _mod_kernel,
        grid=(DEPTH, width // MOD_TN),
        in_specs=[
            pl.BlockSpec((n, D_MODEL), lambda l, j: (0, 0)),
            pl.BlockSpec((1, D_MODEL, MOD_TN), lambda l, j: (l, 0, j)),
            pl.BlockSpec((1, 1, MOD_TN), lambda l, j: (l, 0, j)),
        ],
        out_specs=pl.BlockSpec((1, n, MOD_TN), lambda l, j: (l, 0, j)),
        out_shape=jax.ShapeDtypeStruct((DEPTH, n, width), F32),
        compiler_params=_params(("arbitrary", "arbitrary")),
        name="ada_modulation",
    )(c_all, ada_w, ada_b.reshape(DEPTH, 1, width))


def _ffn_kernel(x_ref, modp_ref, mods_ref, g_ref, win_ref, wo_ref, o_ref, h_s, a_s):
    x = x_ref[...]
    h_s[...] = _modulated(x, g_ref, modp_ref, mods_ref)
    for c in range(N_FFN_CHUNKS):
        h = h_s[...]
        gate = _dot(h, win_ref[:, c * FFN_CHUNK:(c + 1) * FFN_CHUNK])
        up = _dot(h, win_ref[:, FFN_DIM + c * FFN_CHUNK:FFN_DIM + (c + 1) * FFN_CHUNK])
        a_s[:, c * FFN_CHUNK:(c + 1) * FFN_CHUNK] = (gate * jax.nn.sigmoid(gate) * up).astype(BF16)
    y = _dot(a_s[...], wo_ref[...])
    o_ref[...] = x + (HALF_STEP * _mod_vec(modp_ref, mods_ref, 2)) * y


def _ffn_sublayer(x, modp, mods, g, w_in, w_out):
    row = pl.BlockSpec((TM, D_MODEL), lambda i: (i, 0))
    return pl.pallas_call(
        _ffn_kernel,
        grid=(N_TILES,),
        in_specs=[row] + _mod_specs() + [
            _resident((1, D_MODEL)),
            _resident((D_MODEL, 2 * FFN_DIM)),
            _resident((FFN_DIM, D_MODEL)),
        ],
        out_specs=row,
        out_shape=jax.ShapeDtypeStruct((ROWS, D_MODEL), F32),
        scratch_shapes=[pltpu.VMEM((TM, D_MODEL), BF16), pltpu.VMEM((TM, FFN_DIM), BF16)],
        compiler_params=_params(("arbitrary",)),
        name="ffn_sublayer",
    )(x, modp, mods, g, w_in, w_out)


def _rope(t, table):
    w = t * table
    return w + pltpu.roll(w, ROPE_DIM, axis=1)


def _mla_proj_kernel(x_ref, modp_ref, mods_ref, g_ref, win_ref, qn_ref, wuq_ref, kvn_ref,
                     wukt_ref, wuk_ref, wuvt_ref, tab_ref,
                     lat_ref, kr_ref, qp_ref, kp_ref, vt_ref, qs_ref):
    is_p = pl.program_id(0) < N_PTILES
    h = _modulated(x_ref[...], g_ref, modp_ref, mods_ref)
    z = _dot(h, win_ref[...])
    table = tab_ref[...]
    k_rope = _rope(z[:, Q_RANK + KV_RANK:], table)[:, :ROPE_DIM]
    latent = _rms(z[:, Q_RANK:Q_RANK + KV_RANK]) * kvn_ref[...]
    lat_ref[...] = latent
    kr_ref[...] = k_rope
    qn = (_rms(z[:, :Q_RANK]) * qn_ref[...]).astype(BF16)
    q_all = _dot(qn, wuq_ref[...])

    def q_rope(hd):
        t = q_all[:, hd * Q_HEAD_COLS + NOPE_DIM:(hd + 1) * Q_HEAD_COLS]
        return (_rope(t, table)[:, :ROPE_DIM] * ATTN_SCALE).astype(BF16)

    @pl.when(is_p)
    def _():
        lat_b = latent.astype(BF16)
        k_rope_b = k_rope.astype(BF16)
        k_nope = _dot(lat_b, wuk_ref[...])
        v_t = _dot_nt(wuvt_ref[...], lat_b).astype(BF16)
        for kb in range(TM // TK):
            vt_ref[kb] = v_t[:, kb * TK:(kb + 1) * TK]
        for hd in range(N_HEADS):
            q_nope = q_all[:, hd * Q_HEAD_COLS:hd * Q_HEAD_COLS + NOPE_DIM]
            qp_ref[hd, :, :NOPE_DIM] = (q_nope * ATTN_SCALE).astype(BF16)
            qp_ref[hd, :, NOPE_DIM:] = q_rope(hd)
            kp_ref[hd, :, :NOPE_DIM] = k_nope[:, hd * NOPE_DIM:(hd + 1) * NOPE_DIM].astype(BF16)
            kp_ref[hd, :, NOPE_DIM:] = k_rope_b

    @pl.when(jnp.logical_not(is_p))
    def _():
        for hd in range(N_HEADS):
            q_nope = q_all[:, hd * Q_HEAD_COLS:hd * Q_HEAD_COLS + NOPE_DIM].astype(BF16)
            q_lat = _dot(q_nope, wukt_ref[hd]) * ATTN_SCALE
            qs_ref[hd, :, :KV_RANK] = q_lat.astype(BF16)
            qs_ref[hd, :, KV_RANK:] = q_rope(hd)


def _mla_project(x, modp, mods, g, w_in_ext, q_norm, w_uq_ext, kv_norm, w_uk_t, w_uk_flat,
                 w_uv_t, table):
    row = pl.BlockSpec((TM, D_MODEL), lambda i: (i, 0))
    return pl.pallas_call(
        _mla_proj_kernel,
        grid=(N_TILES,),
        in_specs=[row] + _mod_specs() + [
            _resident((1, D_MODEL)),
            _resident(w_in_ext.shape),
            _resident((1, Q_RANK)),
            _resident(w_uq_ext.shape),
            _resident((1, KV_RANK)),
            _resident(w_uk_t.shape),
            _resident(w_uk_flat.shape),
            _resident(w_uv_t.shape),
            pl.BlockSpec((TM, 2 * ROPE_DIM), lambda i: (i, 0)),
        ],
        out_specs=[
            pl.BlockSpec((TM, KV_RANK), lambda i: (i, 0)),
            pl.BlockSpec((TM, ROPE_DIM), lambda i: (i, 0)),
            pl.BlockSpec((N_HEADS, TM, HEAD_DIM), lambda i: (0, _ptile(i), 0)),
            pl.BlockSpec((N_HEADS, TM, HEAD_DIM), lambda i: (0, _ptile(i), 0)),
            pl.BlockSpec((TM // TK, N_HEADS * V_DIM, TK), lambda i: (_ptile(i), 0, 0)),
            pl.BlockSpec((N_HEADS, TM, QK_DIM), lambda i: (0, _stile(i), 0)),
        ],
        out_shape=[
            jax.ShapeDtypeStruct((ROWS, KV_RANK), F32),
            jax.ShapeDtypeStruct((ROWS, ROPE_DIM), F32),
            jax.ShapeDtypeStruct((N_HEADS, P_ROWS, HEAD_DIM), BF16),
            jax.ShapeDtypeStruct((N_HEADS, P_ROWS, HEAD_DIM), BF16),
            jax.ShapeDtypeStruct((P_ROWS // TK, N_HEADS * V_DIM, TK), BF16),
            jax.ShapeDtypeStruct((N_HEADS, S_ROWS, QK_DIM), BF16),
        ],
        compiler_params=_params(("arbitrary",)),
        name="mla_project",
    )(x, modp, mods, g, w_in_ext, q_norm, w_uq_ext, kv_norm, w_uk_t, w_uk_flat, w_uv_t, table)


def _attn_prompt_kernel(q_ref, k_ref, vt_ref, o_ref, m_s, l_s, acc_s):
    qi = pl.program_id(1)
    m_s[...] = jnp.full_like(m_s, -jnp.inf)
    l_s[...] = jnp.zeros_like(l_s)
    acc_s[...] = jnp.zeros_like(acc_s)

    def step(kb0, n_kb, masked):
        keys = n_kb * TK
        off = pl.multiple_of(kb0 * TK, TK)
        for hd in range(N_HEADS):
            s = _dot_nt(k_ref[hd, pl.ds(off, keys), :], q_ref[hd])
            if masked:
                k_pos = off + lax.broadcasted_iota(jnp.int32, (keys, TQ), 0)
                q_pos = qi * TQ + lax.broadcasted_iota(jnp.int32, (keys, TQ), 1)
                s = jnp.where(k_pos <= q_pos, s, -jnp.inf)
            m_old = m_s[hd]
            m_new = jnp.maximum(m_old, jnp.max(s, axis=0, keepdims=True))
            alpha = jnp.exp(m_old - m_new)
            p = jnp.exp(s - m_new)
            l_s[hd] = alpha * l_s[hd] + jnp.sum(p, axis=0, keepdims=True)
            p = p.astype(BF16)
            pv = _dot(vt_ref[kb0, hd * V_DIM:(hd + 1) * V_DIM, :], p[:TK])
            for j in range(1, n_kb):
                pv += _dot(vt_ref[kb0 + j, hd * V_DIM:(hd + 1) * V_DIM, :], p[j * TK:(j + 1) * TK])
            acc_s[hd] = alpha * acc_s[hd] + pv
            m_s[hd] = m_new

    def body(j, carry):
        step(2 * j, 2, False)
        return carry

    lax.fori_loop(0, qi // 2, body, 0)

    @pl.when(qi % 2 == 1)
    def _():
        step(qi - 1, 1, False)

    step(qi, 1, True)
    for hd in range(N_HEADS):
        o_t = acc_s[hd] / l_s[hd]
        o_ref[:, hd * V_DIM:(hd + 1) * V_DIM] = o_t.T.astype(BF16)


def _attn_prompt(q_p, k_p, v_t):
    nq = SEQ // TQ
    return pl.pallas_call(
        _attn_prompt_kernel,
        grid=(BATCH, nq),
        in_specs=[
            pl.BlockSpec((N_HEADS, TQ, HEAD_DIM), lambda b, i: (0, b * nq + i, 0)),
            pl.BlockSpec((N_HEADS, SEQ, HEAD_DIM), lambda b, i: (0, b, 0)),
            pl.BlockSpec((N_KBLOCKS, N_HEADS * V_DIM, TK), lambda b, i: (b, 0, 0)),
        ],
        out_specs=pl.BlockSpec((TQ, N_HEADS * V_DIM), lambda b, i: (b * nq + i, 0)),
        out_shape=jax.ShapeDtypeStruct((P_ROWS, N_HEADS * V_DIM), BF16),
        scratch_shapes=[pltpu.VMEM((N_HEADS, 1, TQ), F32), pltpu.VMEM((N_HEADS, 1, TQ), F32),
                        pltpu.VMEM((N_HEADS, V_DIM, TQ), F32)],
        compiler_params=_params(("arbitrary", "arbitrary")),
        name="attn_prompt",
    )(q_p, k_p, v_t)


def _attn_sample_kernel(pt_ref, q_ref, ln_ref, rn_ref, *refs):
    lat_pages = refs[:PAGES_PER_STEP]
    rope_pages = refs[PAGES_PER_STEP:2 * PAGES_PER_STEP]
    o_ref, m_s, l_s, acc_s, kl_s, kr_s = refs[2 * PAGES_PER_STEP:]
    step = pl.program_id(1)
    rows = N_HEADS * DEC_SEQ

    @pl.when(step == 0)
    def _():
        m_s[...] = jnp.full_like(m_s, -jnp.inf)
        l_s[...] = jnp.zeros_like(l_s)
        acc_s[...] = jnp.zeros_like(acc_s)

    q = q_ref[0]
    q_lat = q[:, :KV_RANK]
    q_rope = q[:, KV_RANK:]

    def update(s, v):
        m_old = m_s[...]
        m_new = jnp.maximum(m_old, jnp.max(s, axis=-1, keepdims=True))
        alpha = jnp.exp(m_old - m_new)
        p = jnp.exp(s - m_new)
        l_s[...] = alpha * l_s[...] + jnp.sum(p, axis=-1, keepdims=True)
        acc_s[...] = alpha * acc_s[...] + _dot(p.astype(BF16), v)
        m_s[...] = m_new

    for p in range(PAGES_PER_STEP):
        kl_s[p * PAGE_SIZE:(p + 1) * PAGE_SIZE, :] = lat_pages[p][0, 0].astype(BF16)
        kr_s[:, p * PAGE_SIZE:(p + 1) * PAGE_SIZE] = rope_pages[p][0, 0].astype(BF16)
    k_lat = kl_s[...]
    update(_dot_nt(q_lat, k_lat) + _dot(q_rope, kr_s[...]), k_lat)

    @pl.when(step == N_PAGE_STEPS - 1)
    def _():
        new_lat = ln_ref[...].astype(BF16)
        new_rope = rn_ref[...].astype(BF16)
        s = _dot_nt(q_lat, new_lat) + _dot_nt(q_rope, new_rope)
        q_pos = lax.broadcasted_iota(jnp.int32, (rows, DEC_SEQ), 0) & (DEC_SEQ - 1)
        k_pos = lax.broadcasted_iota(jnp.int32, (rows, DEC_SEQ), 1)
        update(jnp.where(k_pos <= q_pos, s, -jnp.inf), new_lat)
        ctx = acc_s[...] / l_s[...]
        for hd in range(N_HEADS):
            o_ref[0, :, hd * KV_RANK:(hd + 1) * KV_RANK] = ctx[hd * DEC_SEQ:(hd + 1) * DEC_SEQ]


def _attn_sample(layer, page_table, q_s, lat, k_rope, cache_lat, cache_rope_t):
    rows = N_HEADS * DEC_SEQ
    new_row0 = P_ROWS // DEC_SEQ

    def page_spec(shape, p):
        return pl.BlockSpec(
            (1, 1) + shape,
            lambda b, s, pt: (layer, pt[b, s * PAGES_PER_STEP + p], 0, 0))

    in_specs = [
        pl.BlockSpec((1, rows, QK_DIM), lambda b, s, pt: (b, 0, 0)),
        pl.BlockSpec((DEC_SEQ, KV_RANK), lambda b, s, pt: (new_row0 + b, 0)),
        pl.BlockSpec((DEC_SEQ, ROPE_DIM), lambda b, s, pt: (new_row0 + b, 0)),
    ]
    in_specs += [page_spec((PAGE_SIZE, KV_RANK), p) for p in range(PAGES_PER_STEP)]
    in_specs += [page_spec((ROPE_DIM, PAGE_SIZE), p) for p in range(PAGES_PER_STEP)]
    keys = PAGES_PER_STEP * PAGE_SIZE
    return pl.pallas_call(
        _attn_sample_kernel,
        grid_spec=pltpu.PrefetchScalarGridSpec(
            num_scalar_prefetch=1,
            grid=(DEC_BATCH, N_PAGE_STEPS),
            in_specs=in_specs,
            out_specs=pl.BlockSpec((1, DEC_SEQ, N_HEADS * KV_RANK), lambda b, s, pt: (b, 0, 0)),
            scratch_shapes=[pltpu.VMEM((rows, 1), F32), pltpu.VMEM((rows, 1), F32),
                            pltpu.VMEM((rows, KV_RANK), F32),
                            pltpu.VMEM((keys, KV_RANK), BF16), pltpu.VMEM((ROPE_DIM, keys), BF16)],
        ),
        out_shape=jax.ShapeDtypeStruct((DEC_BATCH, DEC_SEQ, N_HEADS * KV_RANK), F32),
        compiler_params=_params(("arbitrary", "arbitrary")),
        name="attn_sample",
    )(page_table, q_s, lat, k_rope, *([cache_lat] * PAGES_PER_STEP),
      *([cache_rope_t] * PAGES_PER_STEP))


def _mla_out_kernel(x_ref, modp_ref, mods_ref, op_ref, cs_ref, wuv_ref, wo_ref, o_ref, o_s):
    is_p = pl.program_id(0) < N_PTILES

    @pl.when(is_p)
    def _():
        o_s[...] = op_ref[...]

    @pl.when(jnp.logical_not(is_p))
    def _():
        for hd in range(N_HEADS):
            ctx = cs_ref[:, hd * KV_RANK:(hd + 1) * KV_RANK].astype(BF16)
            o_s[:, hd * V_DIM:(hd + 1) * V_DIM] = _dot(ctx, wuv_ref[hd]).astype(BF16)

    y = _dot(o_s[...], wo_ref[...])
    o_ref[...] = x_ref[...] + _mod_vec(modp_ref, mods_ref, 2) * y


def _mla_output(x, modp, mods, o_p, ctx_s, w_uv_h, w_o):
    row = pl.BlockSpec((TM, D_MODEL), lambda i: (i, 0))
    return pl.pallas_call(
        _mla_out_kernel,
        grid=(N_TILES,),
        in_specs=[row] + _mod_specs() + [
            pl.BlockSpec((TM, N_HEADS * V_DIM), lambda i: (_ptile(i), 0)),
            pl.BlockSpec((TM, N_HEADS * KV_RANK), lambda i: (_stile(i), 0)),
            _resident(w_uv_h.shape),
            _resident(w_o.shape),
        ],
        out_specs=row,
        out_shape=jax.ShapeDtypeStruct((ROWS, D_MODEL), F32),
        scratch_shapes=[pltpu.VMEM((TM, N_HEADS * V_DIM), BF16)],
        compiler_params=_params(("arbitrary",)),
        name="mla_output",
    )(x, modp, mods, o_p, ctx_s, w_uv_h, w_o)


def _gmlp_kernel(x_ref, modp_ref, mods_ref, g_ref, win_ref, bin_ref, lng_ref, lnb_ref,
                 wmp_ref, wms_ref, bp_ref, bs_ref, wout_ref, o_ref, v_ref, vb_s, t_s):
    is_p = pl.program_id(0) < N_PTILES
    x = x_ref[...]
    h = _modulated(x, g_ref, modp_ref, mods_ref)
    z = jax.nn.gelu(_dot(h, win_ref[...]) + bin_ref[...])
    v = z[:, GMLP_DIM:]
    mu = jnp.mean(v, axis=-1, keepdims=True)
    var = jnp.mean(jnp.square(v - mu), axis=-1, keepdims=True)
    vn = (v - mu) * lax.rsqrt(var + EPS) * lng_ref[...] + lnb_ref[...]
    v_ref[...] = vn
    vb_s[...] = vn.astype(BF16)
    r = lax.broadcasted_iota(jnp.int32, (CHUNK, CHUNK), 0)
    c = lax.broadcasted_iota(jnp.int32, (CHUNK, CHUNK), 1)
    same_seq = jnp.logical_or(is_p, jnp.bitwise_xor(r, c) < DEC_SEQ)
    keep = jnp.logical_and(r >= c, same_seq)
    bias = jnp.where(is_p, bp_ref[...], bs_ref[...])
    for g in range(GMLP_GROUPS):
        cols = slice(g * GROUP_DIM, (g + 1) * GROUP_DIM)
        w = jnp.where(keep, jnp.where(is_p, wmp_ref[g], wms_ref[g]), 0.0).astype(BF16)
        for ch in range(TM // CHUNK):
            rws = slice(ch * CHUNK, (ch + 1) * CHUNK)
            mixed = _dot(w, vb_s[rws, cols]) + bias[:, cols]
            t_s[rws, cols] = (z[rws, cols] * mixed).astype(BF16)
    y = _dot(t_s[...], wout_ref[...])
    o_ref[...] = x + _mod_vec(modp_ref, mods_ref, 2) * y


def _gmlp_sublayer(x, modp, mods, g, w_in, b_in, ln_g, ln_b, wm_p, wm_s, bias_p, bias_s, w_out):
    row = pl.BlockSpec((TM, D_MODEL), lambda i: (i, 0))
    return pl.pallas_call(
        _gmlp_kernel,
        grid=(N_TILES,),
        in_specs=[row] + _mod_specs() + [
            _resident((1, D_MODEL)),
            _resident(w_in.shape),
            _resident((1, 2 * GMLP_DIM)),
            _resident((1, GMLP_DIM)),
            _resident((1, GMLP_DIM)),
            _resident(wm_p.shape),
            _resident(wm_s.shape),
            _resident(bias_p.shape),
            _resident(bias_s.shape),
            _resident(w_out.shape),
        ],
        out_specs=[row, pl.BlockSpec((TM, GMLP_DIM), lambda i: (i, 0))],
        out_shape=[jax.ShapeDtypeStruct((ROWS, D_MODEL), F32),
                   jax.ShapeDtypeStruct((ROWS, GMLP_DIM), F32)],
        scratch_shapes=[pltpu.VMEM((TM, GMLP_DIM), BF16), pltpu.VMEM((TM, GMLP_DIM), BF16)],
        compiler_params=_params(("arbitrary",)),
        name="gmlp_sublayer",
    )(x, modp, mods, g, w_in, b_in, ln_g, ln_b, wm_p, wm_s, bias_p, bias_s, w_out)


def _final_kernel(x_ref, g_ref, o_ref):
    o_ref[...] = _rms(x_ref[...]) * g_ref[...]


def _final_norm(x, g):
    row = pl.BlockSpec((TM, D_MODEL), lambda i: (i, 0))
    return pl.pallas_call(
        _final_kernel,
        grid=(N_TILES,),
        in_specs=[row, _resident((1, D_MODEL))],
        out_specs=row,
        out_shape=jax.ShapeDtypeStruct((ROWS, D_MODEL), F32),
        compiler_params=_params(("arbitrary",)),
        name="final_norm",
    )(x, g)


def _rope_table():
    half = ROPE_DIM // 2
    freqs = jnp.power(ROPE_THETA, -jnp.arange(half, dtype=F32) / half)
    pos_p = jnp.tile(jnp.arange(SEQ, dtype=F32), BATCH)
    pos_s = jnp.tile(jnp.arange(DEC_SEQ, dtype=F32) + PAST_LEN, DEC_BATCH)
    ang = jnp.concatenate([pos_p, pos_s])[:, None] * freqs[None, :]
    cos, sin = jnp.cos(ang), jnp.sin(ang)
    return jnp.concatenate([cos, cos, -sin, sin], axis=-1)


def _with_rotated(w):
    half = ROPE_DIM // 2
    return jnp.concatenate([w, w[..., -half:], w[..., -ROPE_DIM:-half]], axis=-1)


def _split_mods(mod_l):
    out = []
    for k in range(N_SUB):
        modp = mod_l[:BATCH, k]
        mods = jnp.repeat(mod_l[BATCH:, k], DEC_SEQ, axis=0).transpose(1, 0, 2)
        out.append((modp, mods))
    return out


def kernel(x_prompt, x_sample, cache_kv_latent, cache_k_rope, page_table, c_prompt, c_sample,
           ada_w, ada_b, norm_g, ffn_w_in, ffn_w_out,
           a_w_in, a_q_norm, a_w_uq, a_kv_norm, a_w_uk, a_w_uv, a_w_o,
           b_w_in, b_b_in, b_ln_g, b_ln_b, b_w_s, b_b_s, b_w_out, final_g):
    x = jnp.concatenate([x_prompt.reshape(P_ROWS, D_MODEL), x_sample.reshape(S_ROWS, D_MODEL)])
    c_all = jnp.concatenate([c_prompt, c_sample])
    mod = _ada_modulation(c_all, ada_w, ada_b).reshape(DEPTH, BATCH + DEC_BATCH, N_SUB, 3, D_MODEL)
    table = _rope_table()
    w_in = ffn_w_in.astype(BF16)
    w_out = ffn_w_out.astype(BF16)
    cache_rope_t = cache_k_rope.transpose(0, 1, 3, 2)

    lat_all, rope_all, v_all = [], [], []
    for i in range(DEPTH):
        mods_i = _split_mods(mod[i])
        j = i // 2
        x = _ffn_sublayer(x, *mods_i[0], norm_g[i, 0][None], w_in[i, 0], w_out[i, 0])
        if i % 2 == 0:
            w_in_ext = _with_rotated(a_w_in[j]).astype(BF16)
            w_uq_ext = _with_rotated(a_w_uq[j]).reshape(Q_RANK, N_HEADS * Q_HEAD_COLS).astype(BF16)
            w_uk_t = a_w_uk[j].transpose(1, 2, 0).astype(BF16)
            w_uk_flat = a_w_uk[j].reshape(KV_RANK, N_HEADS * NOPE_DIM).astype(BF16)
            w_uv_h = a_w_uv[j].transpose(1, 0, 2).astype(BF16)
            w_uv_t = a_w_uv[j].reshape(KV_RANK, N_HEADS * V_DIM).T.astype(BF16)
            lat, k_rope, q_p, k_p, v_t, q_s = _mla_project(
                x, *mods_i[1], norm_g[i, 1][None], w_in_ext, a_q_norm[j][None], w_uq_ext,
                a_kv_norm[j][None], w_uk_t, w_uk_flat, w_uv_t, table)
            o_p = _attn_prompt(q_p, k_p, v_t)
            q_s = q_s.reshape(N_HEADS, DEC_BATCH, DEC_SEQ, QK_DIM)
            q_s = q_s.transpose(1, 0, 2, 3).reshape(DEC_BATCH, N_HEADS * DEC_SEQ, QK_DIM)
            ctx_s = _attn_sample(j, page_table, q_s, lat, k_rope, cache_kv_latent, cache_rope_t)
            x = _mla_output(x, *mods_i[1], o_p, ctx_s.reshape(S_ROWS, N_HEADS * KV_RANK),
                            w_uv_h, a_w_o[j].astype(BF16))
            lat_all.append(lat)
            rope_all.append(k_rope)
        else:
            wm_s = jnp.tile(b_w_s[j][:, :DEC_SEQ, :DEC_SEQ], (1, CHUNK // DEC_SEQ, CHUNK // DEC_SEQ))
            bias_p = jnp.repeat(b_b_s[j].T, GROUP_DIM, axis=1)
            bias_s = jnp.tile(bias_p[:DEC_SEQ], (CHUNK // DEC_SEQ, 1))
            x, v = _gmlp_sublayer(
                x, *mods_i[1], norm_g[i, 1][None], b_w_in[j].astype(BF16), b_b_in[j][None],
                b_ln_g[j][None], b_ln_b[j][None], b_w_s[j], wm_s, bias_p, bias_s,
                b_w_out[j].astype(BF16))
            v_all.append(v)
        x = _ffn_sublayer(x, *mods_i[2], norm_g[i, 2][None], w_in[i, 1], w_out[i, 1])

    y = _final_norm(x, final_g[None])

    def prompt_part(a, width):
        return a[:P_ROWS].reshape(BATCH, SEQ, width)

    def sample_part(a, width):
        return a[P_ROWS:].reshape(DEC_BATCH, DEC_SEQ, width)

    return (
        prompt_part(y, D_MODEL),
        sample_part(y, D_MODEL),
        jnp.stack([prompt_part(a, KV_RANK) for a in lat_all]),
        jnp.stack([prompt_part(a, ROPE_DIM) for a in rope_all]),
        jnp.stack([sample_part(a, KV_RANK) for a in lat_all]),
        jnp.stack([sample_part(a, ROPE_DIM) for a in rope_all]),
        jnp.stack([prompt_part(a, GMLP_DIM)[:, SEQ - CHUNK:] for a in v_all]),
        jnp.stack([sample_part(a, GMLP_DIM) for a in v_all]),
    )
```

```python
import functools

import jax
import jax.numpy as jnp
from jax import lax
from jax.experimental import pallas as pl
from jax.experimental.pallas import tpu as pltpu

D_MODEL = 1024
BATCH = 8
SEQ = 2048
DEPTH = 4
DEC_BATCH = 128
DEC_SEQ = 8
PAST_LEN = 8192
PAGE_SIZE = 128
N_PAGES = PAST_LEN // PAGE_SIZE
N_SUB = 3
HALF_STEP = 0.5
EPS = 1e-6
N_HEADS = 8
NOPE_DIM = 128
ROPE_DIM = 64
V_DIM = 128
Q_RANK = 384
KV_RANK = 256
ROPE_THETA = 10000.0
ATTN_SCALE = (NOPE_DIM + ROPE_DIM) ** -0.5
CHUNK = 128
GMLP_DIM = D_MODEL
GMLP_GROUPS = 8
GROUP_DIM = GMLP_DIM // GMLP_GROUPS
FFN_DIM = 2816

P_ROWS = BATCH * SEQ
S_ROWS = DEC_BATCH * DEC_SEQ
ROWS = P_ROWS + S_ROWS
TM = 512
N_TILES = ROWS // TM
N_PTILES = P_ROWS // TM
TILES_PER_SEQ = SEQ // TM
FFN_CHUNK = 256
N_FFN_CHUNKS = FFN_DIM // FFN_CHUNK
QK_DIM = KV_RANK + ROPE_DIM
HEAD_DIM = NOPE_DIM + ROPE_DIM
Q_HEAD_COLS = 2 * NOPE_DIM
TQ = 256
TK = 256
N_KBLOCKS = SEQ // TK
MOD_TN = 1152
VMEM_LIMIT = 56 * 1024 * 1024

F32 = jnp.float32
BF16 = jnp.bfloat16


def _params(sem):
    return pltpu.CompilerParams(dimension_semantics=sem, vmem_limit_bytes=VMEM_LIMIT)


def _resident(shape):
    nd = len(shape)
    return pl.BlockSpec(shape, lambda *_: (0,) * nd, pipeline_mode=pl.Buffered(1))


def _ptile(i):
    return jnp.minimum(i, N_PTILES - 1)


def _stile(i):
    return jnp.maximum(i - N_PTILES, 0)


def _mod_specs():
    modp = pl.BlockSpec((1, 3, D_MODEL),
                        lambda i: (jnp.minimum(i // TILES_PER_SEQ, BATCH - 1), 0, 0))
    mods = pl.BlockSpec((3, TM, D_MODEL), lambda i: (0, _stile(i), 0))
    return [modp, mods]


def _mod_vec(modp_ref, mods_ref, j):
    is_p = pl.program_id(0) < N_PTILES
    return jnp.where(is_p, modp_ref[0, j:j + 1, :], mods_ref[j])


def _rms(x):
    return x * lax.rsqrt(jnp.mean(x * x, axis=-1, keepdims=True) + EPS)


def _modulated(x, g_ref, modp_ref, mods_ref):
    shift = _mod_vec(modp_ref, mods_ref, 0)
    scale = _mod_vec(modp_ref, mods_ref, 1)
    return (_rms(x) * g_ref[...] * (1 + scale) + shift).astype(BF16)


def _dot(a, b):
    return jnp.dot(a, b, preferred_element_type=F32)


def _dot_nt(a, b):
    return lax.dot_general(a, b, (((1,), (1,)), ((), ())), preferred_element_type=F32)


def _mod_kernel(c_ref, w_ref, b_ref, o_ref):
    c = c_ref[...]
    a = (c * jax.nn.sigmoid(c)).astype(BF16)
    o_ref[0] = _dot(a, w_ref[0].astype(BF16)) + b_ref[0]


def _ada_modulation(c_all, ada_w, ada_b):
    n = c_all.shape[0]
    width = N_SUB * 3 * D_MODEL
    return pl.pallas_call(
        _mod_kernel,
        grid=(DEPTH, width // MOD_TN),
        in_specs=[
            pl.BlockSpec((n, D_MODEL), lambda l, j: (0, 0)),
            pl.BlockSpec((1, D_MODEL, MOD_TN), lambda l, j: (l, 0, j)),
            pl.BlockSpec((1, 1, MOD_TN), lambda l, j: (l, 0, j)),
        ],
        out_specs=pl.BlockSpec((1, n, MOD_TN), lambda l, j: (l, 0, j)),
        out_shape=jax.ShapeDtypeStruct((DEPTH, n, width), F32),
        compiler_params=_params(("arbitrary", "arbitrary")),
        name="ada_modulation",
    )(c_all, ada_w, ada_b.reshape(DEPTH, 1, width))


def _ffn_kernel(x_ref, modp_ref, mods_ref, g_ref, win_ref, wo_ref, o_ref, h_s, a_s):
    x = x_ref[...]
    h_s[...] = _modulated(x, g_ref, modp_ref, mods_ref)
    for c in range(N_FFN_CHUNKS):
        h = h_s[...]
        gate = _dot(h, win_ref[:, c * FFN_CHUNK:(c + 1) * FFN_CHUNK])
        up = _dot(h, win_ref[:, FFN_DIM + c * FFN_CHUNK:FFN_DIM + (c + 1) * FFN_CHUNK])
        a_s[:, c * FFN_CHUNK:(c + 1) * FFN_CHUNK] = (gate * jax.nn.sigmoid(gate) * up).astype(BF16)
    y = _dot(a_s[...], wo_ref[...])
    o_ref[...] = x + (HALF_STEP * _mod_vec(modp_ref, mods_ref, 2)) * y


def _ffn_sublayer(x, modp, mods, g, w_in, w_out):
    row = pl.BlockSpec((TM, D_MODEL), lambda i: (i, 0))
    return pl.pallas_call(
        _ffn_kernel,
        grid=(N_TILES,),
        in_specs=[row] + _mod_specs() + [
            _resident((1, D_MODEL)),
            _resident((D_MODEL, 2 * FFN_DIM)),
            _resident((FFN_DIM, D_MODEL)),
        ],
        out_specs=row,
        out_shape=jax.ShapeDtypeStruct((ROWS, D_MODEL), F32),
        scratch_shapes=[pltpu.VMEM((TM, D_MODEL), BF16), pltpu.VMEM((TM, FFN_DIM), BF16)],
        compiler_params=_params(("arbitrary",)),
        name="ffn_sublayer",
    )(x, modp, mods, g, w_in, w_out)


def _rope(t, table):
    w = t * table
    return w + pltpu.roll(w, ROPE_DIM, axis=1)


def _mla_proj_kernel(x_ref, modp_ref, mods_ref, g_ref, win_ref, qn_ref, wuq_ref, kvn_ref,
                     wukt_ref, wuk_ref, wuvt_ref, tab_ref,
                     lat_ref, kr_ref, qp_ref, kp_ref, vt_ref, qs_ref):
    is_p = pl.program_id(0) < N_PTILES
    h = _modulated(x_ref[...], g_ref, modp_ref, mods_ref)
    z = _dot(h, win_ref[...])
    table = tab_ref[...]
    k_rope = _rope(z[:, Q_RANK + KV_RANK:], table)[:, :ROPE_DIM]
    latent = _rms(z[:, Q_RANK:Q_RANK + KV_RANK]) * kvn_ref[...]
    lat_ref[...] = latent
    kr_ref[...] = k_rope
    qn = (_rms(z[:, :Q_RANK]) * qn_ref[...]).astype(BF16)
    q_all = _dot(qn, wuq_ref[...])

    def q_rope(hd):
        t = q_all[:, hd * Q_HEAD_COLS + NOPE_DIM:(hd + 1) * Q_HEAD_COLS]
        return (_rope(t, table)[:, :ROPE_DIM] * ATTN_SCALE).astype(BF16)

    @pl.when(is_p)
    def _():
        lat_b = latent.astype(BF16)
        k_rope_b = k_rope.astype(BF16)
        k_nope = _dot(lat_b, wuk_ref[...])
        v_t = _dot_nt(wuvt_ref[...], lat_b).astype(BF16)
        for kb in range(TM // TK):
            vt_ref[kb] = v_t[:, kb * TK:(kb + 1) * TK]
        for hd in range(N_HEADS):
            q_nope = q_all[:, hd * Q_HEAD_COLS:hd * Q_HEAD_COLS + NOPE_DIM]
            qp_ref[hd, :, :NOPE_DIM] = (q_nope * ATTN_SCALE).astype(BF16)
            qp_ref[hd, :, NOPE_DIM:] = q_rope(hd)
            kp_ref[hd, :, :NOPE_DIM] = k_nope[:, hd * NOPE_DIM:(hd + 1) * NOPE_DIM].astype(BF16)
            kp_ref[hd, :, NOPE_DIM:] = k_rope_b

    @pl.when(jnp.logical_not(is_p))
    def _():
        for hd in range(N_HEADS):
            q_nope = q_all[:, hd * Q_HEAD_COLS:hd * Q_HEAD_COLS + NOPE_DIM].astype(BF16)
            q_lat = _dot(q_nope, wukt_ref[hd]) * ATTN_SCALE
            qs_ref[hd, :, :KV_RANK] = q_lat.astype(BF16)
            qs_ref[hd, :, KV_RANK:] = q_rope(hd)


def _mla_project(x, modp, mods, g, w_in_ext, q_norm, w_uq_ext, kv_norm, w_uk_t, w_uk_flat,
                 w_uv_t, table):
    row = pl.BlockSpec((TM, D_MODEL), lambda i: (i, 0))
    return pl.pallas_call(
        _mla_proj_kernel,
        grid=(N_TILES,),
        in_specs=[row] + _mod_specs() + [
            _resident((1, D_MODEL)),
            _resident(w_in_ext.shape),
            _resident((1, Q_RANK)),
            _resident(w_uq_ext.shape),
            _resident((1, KV_RANK)),
            _resident(w_uk_t.shape),
            _resident(w_uk_flat.shape),
            _resident(w_uv_t.shape),
            pl.BlockSpec((TM, 2 * ROPE_DIM), lambda i: (i, 0)),
        ],
        out_specs=[
            pl.BlockSpec((TM, KV_RANK), lambda i: (i, 0)),
            pl.BlockSpec((TM, ROPE_DIM), lambda i: (i, 0)),
            pl.BlockSpec((N_HEADS, TM, HEAD_DIM), lambda i: (0, _ptile(i), 0)),
            pl.BlockSpec((N_HEADS, TM, HEAD_DIM), lambda i: (0, _ptile(i), 0)),
            pl.BlockSpec((TM // TK, N_HEADS * V_DIM, TK), lambda i: (_ptile(i), 0, 0)),
            pl.BlockSpec((N_HEADS, TM, QK_DIM), lambda i: (0, _stile(i), 0)),
        ],
        out_shape=[
            jax.ShapeDtypeStruct((ROWS, KV_RANK), F32),
            jax.ShapeDtypeStruct((ROWS, ROPE_DIM), F32),
            jax.ShapeDtypeStruct((N_HEADS, P_ROWS, HEAD_DIM), BF16),
            jax.ShapeDtypeStruct((N_HEADS, P_ROWS, HEAD_DIM), BF16),
            jax.ShapeDtypeStruct((P_ROWS // TK, N_HEADS * V_DIM, TK), BF16),
            jax.ShapeDtypeStruct((N_HEADS, S_ROWS, QK_DIM), BF16),
        ],
        compiler_params=_params(("arbitrary",)),
        name="mla_project",
    )(x, modp, mods, g, w_in_ext, q_norm, w_uq_ext, kv_norm, w_uk_t, w_uk_flat, w_uv_t, table)


def _attn_prompt_kernel(q_ref, k_ref, vt_ref, o_ref, m_s, l_s, acc_s):
    qi = pl.program_id(1)
    m_s[...] = jnp.full_like(m_s, -jnp.inf)
    l_s[...] = jnp.zeros_like(l_s)
    acc_s[...] = jnp.zeros_like(acc_s)

    def step(kb0, n_kb, masked):
        keys = n_kb * TK
        off = pl.multiple_of(kb0 * TK, TK)
        for hd in range(N_HEADS):
            s = _dot_nt(k_ref[hd, pl.ds(off, keys), :], q_ref[hd])
            if masked:
                k_pos = off + lax.broadcasted_iota(jnp.int32, (keys, TQ), 0)
                q_pos = qi * TQ + lax.broadcasted_iota(jnp.int32, (keys, TQ), 1)
                s = jnp.where(k_pos <= q_pos, s, -jnp.inf)
            m_old = m_s[hd]
            m_new = jnp.maximum(m_old, jnp.max(s, axis=0, keepdims=True))
            alpha = jnp.exp(m_old - m_new)
            p = jnp.exp(s - m_new)
            l_s[hd] = alpha * l_s[hd] + jnp.sum(p, axis=0, keepdims=True)
            p = p.astype(BF16)
            pv = _dot(vt_ref[kb0, hd * V_DIM:(hd + 1) * V_DIM, :], p[:TK])
            for j in range(1, n_kb):
                pv += _dot(vt_ref[kb0 + j, hd * V_DIM:(hd + 1) * V_DIM, :], p[j * TK:(j + 1) * TK])
            acc_s[hd] = alpha * acc_s[hd] + pv
            m_s[hd] = m_new

    def body(j, carry):
        step(2 * j, 2, False)
        return carry

    lax.fori_loop(0, qi // 2, body, 0)

    @pl.when(qi % 2 == 1)
    def _():
        step(qi - 1, 1, False)

    step(qi, 1, True)
    for hd in range(N_HEADS):
        o_t = acc_s[hd] / l_s[hd]
        o_ref[:, hd * V_DIM:(hd + 1) * V_DIM] = o_t.T.astype(BF16)


def _attn_prompt(q_p, k_p, v_t):
    nq = SEQ // TQ
    return pl.pallas_call(
        _attn_prompt_kernel,
        grid=(BATCH, nq),
        in_specs=[
            pl.BlockSpec((N_HEADS, TQ, HEAD_DIM), lambda b, i: (0, b * nq + i, 0)),
            pl.BlockSpec((N_HEADS, SEQ, HEAD_DIM), lambda b, i: (0, b, 0)),
            pl.BlockSpec((N_KBLOCKS, N_HEADS * V_DIM, TK), lambda b, i: (b, 0, 0)),
        ],
        out_specs=pl.BlockSpec((TQ, N_HEADS * V_DIM), lambda b, i: (b * nq + i, 0)),
        out_shape=jax.ShapeDtypeStruct((P_ROWS, N_HEADS * V_DIM), BF16),
        scratch_shapes=[pltpu.VMEM((N_HEADS, 1, TQ), F32), pltpu.VMEM((N_HEADS, 1, TQ), F32),
                        pltpu.VMEM((N_HEADS, V_DIM, TQ), F32)],
        compiler_params=_params(("arbitrary", "arbitrary")),
        name="attn_prompt",
    )(q_p, k_p, v_t)


def _attn_sample_kernel(layer, pt_ref, q_ref, ln_ref, rn_ref, lat_hbm, rope_hbm, o_ref,
                        lat_buf, rope_buf, kl_s, kr_s, sem):
    b = pl.program_id(0)
    slot = lax.rem(b, 2)
    rows = N_HEADS * DEC_SEQ

    def page_copies(seq, sl, p):
        page = pt_ref[seq, p]
        return (pltpu.make_async_copy(lat_hbm.at[layer, page], lat_buf.at[sl, p], sem.at[sl, 0]),
                pltpu.make_async_copy(rope_hbm.at[layer, page], rope_buf.at[sl, p], sem.at[sl, 1]))

    def start_pages(seq, sl):
        def body(p, carry):
            for cp in page_copies(seq, sl, p):
                cp.start()
            return carry
        lax.fori_loop(0, N_PAGES, body, 0)

    @pl.when(b == 0)
    def _():
        start_pages(0, 0)

    @pl.when(b + 1 < DEC_BATCH)
    def _():
        start_pages(b + 1, 1 - slot)

    def wait_body(p, carry):
        for cp in page_copies(b, slot, p):
            cp.wait()
        return carry
    lax.fori_loop(0, N_PAGES, wait_body, 0)

    for p in range(N_PAGES):
        kl_s[p * PAGE_SIZE:(p + 1) * PAGE_SIZE, :] = lat_buf[slot, p].astype(BF16)
        kr_s[:, p * PAGE_SIZE:(p + 1) * PAGE_SIZE] = rope_buf[slot, p].astype(BF16)

    q = q_ref[0]
    q_lat = q[:, :KV_RANK]
    q_rope = q[:, KV_RANK:]
    k_lat = kl_s[...]
    s = _dot_nt(q_lat, k_lat) + _dot(q_rope, kr_s[...])
    new_lat = ln_ref[...].astype(BF16)
    new_rope = rn_ref[...].astype(BF16)
    s_new = _dot_nt(q_lat, new_lat) + _dot_nt(q_rope, new_rope)
    q_pos = lax.broadcasted_iota(jnp.int32, (rows, DEC_SEQ), 0) & (DEC_SEQ - 1)
    k_pos = lax.broadcasted_iota(jnp.int32, (rows, DEC_SEQ), 1)
    s_new = jnp.where(k_pos <= q_pos, s_new, -jnp.inf)
    m = jnp.maximum(jnp.max(s, axis=-1, keepdims=True), jnp.max(s_new, axis=-1, keepdims=True))
    p = jnp.exp(s - m)
    p_new = jnp.exp(s_new - m)
    l = jnp.sum(p, axis=-1, keepdims=True) + jnp.sum(p_new, axis=-1, keepdims=True)
    ctx = (_dot(p.astype(BF16), k_lat) + _dot(p_new.astype(BF16), new_lat)) / l
    for hd in range(N_HEADS):
        o_ref[0, :, hd * KV_RANK:(hd + 1) * KV_RANK] = ctx[hd * DEC_SEQ:(hd + 1) * DEC_SEQ]


def _attn_sample(layer, page_table, q_s, lat, k_rope, cache_lat, cache_rope_t):
    rows = N_HEADS * DEC_SEQ
    new_row0 = P_ROWS // DEC_SEQ
    return pl.pallas_call(
        functools.partial(_attn_sample_kernel, layer),
        grid_spec=pltpu.PrefetchScalarGridSpec(
            num_scalar_prefetch=1,
            grid=(DEC_BATCH,),
            in_specs=[
                pl.BlockSpec((1, rows, QK_DIM), lambda b, pt: (b, 0, 0)),
                pl.BlockSpec((DEC_SEQ, KV_RANK), lambda b, pt: (new_row0 + b, 0)),
                pl.BlockSpec((DEC_SEQ, ROPE_DIM), lambda b, pt: (new_row0 + b, 0)),
                pl.BlockSpec(memory_space=pl.ANY),
                pl.BlockSpec(memory_space=pl.ANY),
            ],
            out_specs=pl.BlockSpec((1, DEC_SEQ, N_HEADS * KV_RANK), lambda b, pt: (b, 0, 0)),
            scratch_shapes=[
                pltpu.VMEM((2, N_PAGES, PAGE_SIZE, KV_RANK), F32),
                pltpu.VMEM((2, N_PAGES, ROPE_DIM, PAGE_SIZE), F32),
                pltpu.VMEM((PAST_LEN, KV_RANK), BF16),
                pltpu.VMEM((ROPE_DIM, PAST_LEN), BF16),
                pltpu.SemaphoreType.DMA((2, 2)),
            ],
        ),
        out_shape=jax.ShapeDtypeStruct((DEC_BATCH, DEC_SEQ, N_HEADS * KV_RANK), F32),
        compiler_params=_params(("arbitrary",)),
        name="attn_sample",
    )(page_table, q_s, lat, k_rope, cache_lat, cache_rope_t)


def _mla_out_kernel(x_ref, modp_ref, mods_ref, op_ref, cs_ref, wuv_ref, wo_ref, o_ref, o_s):
    is_p = pl.program_id(0) < N_PTILES

    @pl.when(is_p)
    def _():
        o_s[...] = op_ref[...]

    @pl.when(jnp.logical_not(is_p))
    def _():
        for hd in range(N_HEADS):
            ctx = cs_ref[:, hd * KV_RANK:(hd + 1) * KV_RANK].astype(BF16)
            o_s[:, hd * V_DIM:(hd + 1) * V_DIM] = _dot(ctx, wuv_ref[hd]).astype(BF16)

    y = _dot(o_s[...], wo_ref[...])
    o_ref[...] = x_ref[...] + _mod_vec(modp_ref, mods_ref, 2) * y


def _mla_output(x, modp, mods, o_p, ctx_s, w_uv_h, w_o):
    row = pl.BlockSpec((TM, D_MODEL), lambda i: (i, 0))
    return pl.pallas_call(
        _mla_out_kernel,
        grid=(N_TILES,),
        in_specs=[row] + _mod_specs() + [
            pl.BlockSpec((TM, N_HEADS * V_DIM), lambda i: (_ptile(i), 0)),
            pl.BlockSpec((TM, N_HEADS * KV_RANK), lambda i: (_stile(i), 0)),
            _resident(w_uv_h.shape),
            _resident(w_o.shape),
        ],
        out_specs=row,
        out_shape=jax.ShapeDtypeStruct((ROWS, D_MODEL), F32),
        scratch_shapes=[pltpu.VMEM((TM, N_HEADS * V_DIM), BF16)],
        compiler_params=_params(("arbitrary",)),
        name="mla_output",
    )(x, modp, mods, o_p, ctx_s, w_uv_h, w_o)


def _gmlp_kernel(x_ref, modp_ref, mods_ref, g_ref, win_ref, bin_ref, lng_ref, lnb_ref,
                 wmp_ref, wms_ref, bp_ref, bs_ref, wout_ref, o_ref, v_ref, vb_s, t_s):
    is_p = pl.program_id(0) < N_PTILES
    x = x_ref[...]
    h = _modulated(x, g_ref, modp_ref, mods_ref)
    z = jax.nn.gelu(_dot(h, win_ref[...]) + bin_ref[...])
    v = z[:, GMLP_DIM:]
    mu = jnp.mean(v, axis=-1, keepdims=True)
    var = jnp.mean(jnp.square(v - mu), axis=-1, keepdims=True)
    vn = (v - mu) * lax.rsqrt(var + EPS) * lng_ref[...] + lnb_ref[...]
    v_ref[...] = vn
    vb_s[...] = vn.astype(BF16)
    r = lax.broadcasted_iota(jnp.int32, (CHUNK, CHUNK), 0)
    c = lax.broadcasted_iota(jnp.int32, (CHUNK, CHUNK), 1)
    same_seq = jnp.logical_or(is_p, jnp.bitwise_xor(r, c) < DEC_SEQ)
    keep = jnp.logical_and(r >= c, same_seq)
    bias = jnp.where(is_p, bp_ref[...], bs_ref[...])
    for g in range(GMLP_GROUPS):
        cols = slice(g * GROUP_DIM, (g + 1) * GROUP_DIM)
        w = jnp.where(keep, jnp.where(is_p, wmp_ref[g], wms_ref[g]), 0.0).astype(BF16)
        for ch in range(TM // CHUNK):
            rws = slice(ch * CHUNK, (ch + 1) * CHUNK)
            mixed = _dot(w, vb_s[rws, cols]) + bias[:, cols]
            t_s[rws, cols] = (z[rws, cols] * mixed).astype(BF16)
    y = _dot(t_s[...], wout_ref[...])
    o_ref[...] = x + _mod_vec(modp_ref, mods_ref, 2) * y


def _gmlp_sublayer(x, modp, mods, g, w_in, b_in, ln_g, ln_b, wm_p, wm_s, bias_p, bias_s, w_out):
    row = pl.BlockSpec((TM, D_MODEL), lambda i: (i, 0))
    return pl.pallas_call(
        _gmlp_kernel,
        grid=(N_TILES,),
        in_specs=[row] + _mod_specs() + [
            _resident((1, D_MODEL)),
            _resident(w_in.shape),
            _resident((1, 2 * GMLP_DIM)),
            _resident((1, GMLP_DIM)),
            _resident((1, GMLP_DIM)),
            _resident(wm_p.shape),
            _resident(wm_s.shape),
            _resident(bias_p.shape),
            _resident(bias_s.shape),
            _resident(w_out.shape),
        ],
        out_specs=[row, pl.BlockSpec((TM, GMLP_DIM), lambda i: (i, 0))],
        out_shape=[jax.ShapeDtypeStruct((ROWS, D_MODEL), F32),
                   jax.ShapeDtypeStruct((ROWS, GMLP_DIM), F32)],
        scratch_shapes=[pltpu.VMEM((TM, GMLP_DIM), BF16), pltpu.VMEM((TM, GMLP_DIM), BF16)],
        compiler_params=_params(("arbitrary",)),
        name="gmlp_sublayer",
    )(x, modp, mods, g, w_in, b_in, ln_g, ln_b, wm_p, wm_s, bias_p, bias_s, w_out)


def _final_kernel(x_ref, g_ref, o_ref):
    o_ref[...] = _rms(x_ref[...]) * g_ref[...]


def _final_norm(x, g):
    row = pl.BlockSpec((TM, D_MODEL), lambda i: (i, 0))
    return pl.pallas_call(
        _final_kernel,
        grid=(N_TILES,),
        in_specs=[row, _resident((1, D_MODEL))],
        out_specs=row,
        out_shape=jax.ShapeDtypeStruct((ROWS, D_MODEL), F32),
        compiler_params=_params(("arbitrary",)),
        name="final_norm",
    )(x, g)


def _rope_table():
    half = ROPE_DIM // 2
    freqs = jnp.power(ROPE_THETA, -jnp.arange(half, dtype=F32) / half)
    pos_p = jnp.tile(jnp.arange(SEQ, dtype=F32), BATCH)
    pos_s = jnp.tile(jnp.arange(DEC_SEQ, dtype=F32) + PAST_LEN, DEC_BATCH)
    ang = jnp.concatenate([pos_p, pos_s])[:, None] * freqs[None, :]
    cos, sin = jnp.cos(ang), jnp.sin(ang)
    return jnp.concatenate([cos, cos, -sin, sin], axis=-1)


def _with_rotated(w):
    half = ROPE_DIM // 2
    return jnp.concatenate([w, w[..., -half:], w[..., -ROPE_DIM:-half]], axis=-1)


def _split_mods(mod_l):
    out = []
    for k in range(N_SUB):
        modp = mod_l[:BATCH, k]
        mods = jnp.repeat(mod_l[BATCH:, k], DEC_SEQ, axis=0).transpose(1, 0, 2)
        out.append((modp, mods))
    return out


def kernel(x_prompt, x_sample, cache_kv_latent, cache_k_rope, page_table, c_prompt, c_sample,
           ada_w, ada_b, norm_g, ffn_w_in, ffn_w_out,
           a_w_in, a_q_norm, a_w_uq, a_kv_norm, a_w_uk, a_w_uv, a_w_o,
           b_w_in, b_b_in, b_ln_g, b_ln_b, b_w_s, b_b_s, b_w_out, final_g):
    x = jnp.concatenate([x_prompt.reshape(P_ROWS, D_MODEL), x_sample.reshape(S_ROWS, D_MODEL)])
    c_all = jnp.concatenate([c_prompt, c_sample])
    mod = _ada_modulation(c_all, ada_w, ada_b).reshape(DEPTH, BATCH + DEC_BATCH, N_SUB, 3, D_MODEL)
    table = _rope_table()
    w_in = ffn_w_in.astype(BF16)
    w_out = ffn_w_out.astype(BF16)
    cache_rope_t = cache_k_rope.transpose(0, 1, 3, 2)

    lat_all, rope_all, v_all = [], [], []
    for i in range(DEPTH):
        mods_i = _split_mods(mod[i])
        j = i // 2
        x = _ffn_sublayer(x, *mods_i[0], norm_g[i, 0][None], w_in[i, 0], w_out[i, 0])
        if i % 2 == 0:
            w_in_ext = _with_rotated(a_w_in[j]).astype(BF16)
            w_uq_ext = _with_rotated(a_w_uq[j]).reshape(Q_RANK, N_HEADS * Q_HEAD_COLS).astype(BF16)
            w_uk_t = a_w_uk[j].transpose(1, 2, 0).astype(BF16)
            w_uk_flat = a_w_uk[j].reshape(KV_RANK, N_HEADS * NOPE_DIM).astype(BF16)
            w_uv_h = a_w_uv[j].transpose(1, 0, 2).astype(BF16)
            w_uv_t = a_w_uv[j].reshape(KV_RANK, N_HEADS * V_DIM).T.astype(BF16)
            lat, k_rope, q_p, k_p, v_t, q_s = _mla_project(
                x, *mods_i[1], norm_g[i, 1][None], w_in_ext, a_q_norm[j][None], w_uq_ext,
                a_kv_norm[j][None], w_uk_t, w_uk_flat, w_uv_t, table)
            o_p = _attn_prompt(q_p, k_p, v_t)
            q_s = q_s.reshape(N_HEADS, DEC_BATCH, DEC_SEQ, QK_DIM)
            q_s = q_s.transpose(1, 0, 2, 3).reshape(DEC_BATCH, N_HEADS * DEC_SEQ, QK_DIM)
            ctx_s = _attn_sample(j, page_table, q_s, lat, k_rope, cache_kv_latent, cache_rope_t)
            x = _mla_output(x, *mods_i[1], o_p, ctx_s.reshape(S_ROWS, N_HEADS * KV_RANK),
                            w_uv_h, a_w_o[j].astype(BF16))
            lat_all.append(lat)
            rope_all.append(k_rope)
        else:
            wm_s = jnp.tile(b_w_s[j][:, :DEC_SEQ, :DEC_SEQ], (1, CHUNK // DEC_SEQ, CHUNK // DEC_SEQ))
            bias_p = jnp.repeat(b_b_s[j].T, GROUP_DIM, axis=1)
            bias_s = jnp.tile(bias_p[:DEC_SEQ], (CHUNK // DEC_SEQ, 1))
            x, v = _gmlp_sublayer(
                x, *mods_i[1], norm_g[i, 1][None], b_w_in[j].astype(BF16), b_b_in[j][None],
                b_ln_g[j][None], b_ln_b[j][None], b_w_s[j], wm_s, bias_p, bias_s,
                b_w_out[j].astype(BF16))
            v_all.append(v)
        x = _ffn_sublayer(x, *mods_i[2], norm_g[i, 2][None], w_in[i, 1], w_out[i, 1])

    y = _final_norm(x, final_g[None])

    def prompt_part(a, width):
        return a[:P_ROWS].reshape(BATCH, SEQ, width)

    def sample_part(a, width):
        return a[P_ROWS:].reshape(DEC_BATCH, DEC_SEQ, width)

    return (
        prompt_part(y, D_MODEL),
        sample_part(y, D_MODEL),
        jnp.stack([prompt_part(a, KV_RANK) for a in lat_all]),
        jnp.stack([prompt_part(a, ROPE_DIM) for a in rope_all]),
        jnp.stack([sample_part(a, KV_RANK) for a in lat_all]),
        jnp.stack([sample_part(a, ROPE_DIM) for a in rope_all]),
        jnp.stack([prompt_part(a, GMLP_DIM)[:, SEQ - CHUNK:] for a in v_all]),
        jnp.stack([sample_part(a, GMLP_DIM) for a in v_all]),
    )
```

```python
import functools

import jax
import jax.numpy as jnp
from jax import lax
from jax.experimental import pallas as pl
from jax.experimental.pallas import tpu as pltpu

D_MODEL = 1024
BATCH = 8
SEQ = 2048
DEPTH = 4
DEC_BATCH = 128
DEC_SEQ = 8
PAST_LEN = 8192
PAGE_SIZE = 128
N_PAGES = PAST_LEN // PAGE_SIZE
N_SUB = 3
HALF_STEP = 0.5
EPS = 1e-6
N_HEADS = 8
NOPE_DIM = 128
ROPE_DIM = 64
V_DIM = 128
Q_RANK = 384
KV_RANK = 256
ROPE_THETA = 10000.0
ATTN_SCALE = (NOPE_DIM + ROPE_DIM) ** -0.5
CHUNK = 128
GMLP_DIM = D_MODEL
GMLP_GROUPS = 8
GROUP_DIM = GMLP_DIM // GMLP_GROUPS
FFN_DIM = 2816

P_ROWS = BATCH * SEQ
S_ROWS = DEC_BATCH * DEC_SEQ
ROWS = P_ROWS + S_ROWS
TM = 512
N_TILES = ROWS // TM
N_PTILES = P_ROWS // TM
TILES_PER_SEQ = SEQ // TM
FFN_CHUNK = 256
N_FFN_CHUNKS = FFN_DIM // FFN_CHUNK
QK_DIM = KV_RANK + ROPE_DIM
HEAD_DIM = NOPE_DIM + ROPE_DIM
Q_HEAD_COLS = 2 * NOPE_DIM
TQ = 256
TK = 256
N_KBLOCKS = SEQ // TK
MOD_TN = 1152
VMEM_LIMIT = 56 * 1024 * 1024

F32 = jnp.float32
BF16 = jnp.bfloat16


def _params(sem):
    return pltpu.CompilerParams(dimension_semantics=sem, vmem_limit_bytes=VMEM_LIMIT)


def _resident(shape):
    nd = len(shape)
    return pl.BlockSpec(shape, lambda *_: (0,) * nd, pipeline_mode=pl.Buffered(1))


def _ptile(i):
    return jnp.minimum(i, N_PTILES - 1)


def _stile(i):
    return jnp.maximum(i - N_PTILES, 0)


def _mod_specs():
    modp = pl.BlockSpec((1, 3, D_MODEL),
                        lambda i: (jnp.minimum(i // TILES_PER_SEQ, BATCH - 1), 0, 0))
    mods = pl.BlockSpec((3, TM, D_MODEL), lambda i: (0, _stile(i), 0))
    return [modp, mods]


def _mod_vec(modp_ref, mods_ref, j):
    is_p = pl.program_id(0) < N_PTILES
    return jnp.where(is_p, modp_ref[0, j:j + 1, :], mods_ref[j])


def _rms(x):
    return x * lax.rsqrt(jnp.mean(x * x, axis=-1, keepdims=True) + EPS)


def _modulated(x, g_ref, modp_ref, mods_ref):
    shift = _mod_vec(modp_ref, mods_ref, 0)
    scale = _mod_vec(modp_ref, mods_ref, 1)
    return (_rms(x) * g_ref[...] * (1 + scale) + shift).astype(BF16)


def _dot(a, b):
    return jnp.dot(a, b, preferred_element_type=F32)


def _dot_nt(a, b):
    return lax.dot_general(a, b, (((1,), (1,)), ((), ())), preferred_element_type=F32)


def _mod_kernel(c_ref, w_ref, b_ref, o_ref):
    c = c_ref[...]
    a = (c * jax.nn.sigmoid(c)).astype(BF16)
    o_ref[0] = _dot(a, w_ref[0].astype(BF16)) + b_ref[0]


def _ada_modulation(c_all, ada_w, ada_b):
    n = c_all.shape[0]
    width = N_SUB * 3 * D_MODEL
    return pl.pallas_call(
        _mod_kernel,
        grid=(DEPTH, width // MOD_TN),
        in_specs=[
            pl.BlockSpec((n, D_MODEL), lambda l, j: (0, 0)),
            pl.BlockSpec((1, D_MODEL, MOD_TN), lambda l, j: (l, 0, j)),
            pl.BlockSpec((1, 1, MOD_TN), lambda l, j: (l, 0, j)),
        ],
        out_specs=pl.BlockSpec((1, n, MOD_TN), lambda l, j: (l, 0, j)),
        out_shape=jax.ShapeDtypeStruct((DEPTH, n, width), F32),
        compiler_params=_params(("arbitrary", "arbitrary")),
        name="ada_modulation",
    )(c_all, ada_w, ada_b.reshape(DEPTH, 1, width))


def _ffn_kernel(n_x, final, *refs):
    x_refs, refs = refs[:n_x], refs[n_x:]
    modp_ref, mods_ref, g_ref, win_ref, wo_ref = refs[:5]
    refs = refs[5:]
    is_p = pl.program_id(0) < N_PTILES
    x = x_refs[0][...] if n_x == 1 else jnp.where(is_p, x_refs[0][...], x_refs[1][...])
    h_s, a_s = refs[-2:]
    h_s[...] = _modulated(x, g_ref, modp_ref, mods_ref)
    for c in range(N_FFN_CHUNKS):
        h = h_s[...]
        gate = _dot(h, win_ref[:, c * FFN_CHUNK:(c + 1) * FFN_CHUNK])
        up = _dot(h, win_ref[:, FFN_DIM + c * FFN_CHUNK:FFN_DIM + (c + 1) * FFN_CHUNK])
        a_s[:, c * FFN_CHUNK:(c + 1) * FFN_CHUNK] = (gate * jax.nn.sigmoid(gate) * up).astype(BF16)
    y = _dot(a_s[...], wo_ref[...])
    out = x + (HALF_STEP * _mod_vec(modp_ref, mods_ref, 2)) * y
    if not final:
        refs[0][...] = out
        return
    fg_ref, yp_ref, ys_ref = refs[:3]
    normed = _rms(out) * fg_ref[...]

    @pl.when(is_p)
    def _():
        yp_ref[...] = normed

    @pl.when(jnp.logical_not(is_p))
    def _():
        ys_ref[...] = normed


def _row_spec(width=D_MODEL):
    return pl.BlockSpec((TM, width), lambda i: (i, 0))


def _prow_spec(width=D_MODEL):
    return pl.BlockSpec((TM, width), lambda i: (_ptile(i), 0))


def _srow_spec(width=D_MODEL):
    return pl.BlockSpec((TM, width), lambda i: (_stile(i), 0))


def _ffn_sublayer(xs, modp, mods, g, w_in, w_out, final_g=None):
    final = final_g is not None
    x_specs = [_row_spec()] if len(xs) == 1 else [_prow_spec(), _srow_spec()]
    in_specs = x_specs + _mod_specs() + [
        _resident((1, D_MODEL)),
        _resident((D_MODEL, 2 * FFN_DIM)),
        _resident((FFN_DIM, D_MODEL)),
    ]
    args = list(xs) + [modp, mods, g, w_in, w_out]
    if final:
        in_specs.append(_resident((1, D_MODEL)))
        args.append(final_g)
        out_specs = [_prow_spec(), _srow_spec()]
        out_shape = [jax.ShapeDtypeStruct((P_ROWS, D_MODEL), F32),
                     jax.ShapeDtypeStruct((S_ROWS, D_MODEL), F32)]
    else:
        out_specs = _row_spec()
        out_shape = jax.ShapeDtypeStruct((ROWS, D_MODEL), F32)
    return pl.pallas_call(
        functools.partial(_ffn_kernel, len(xs), final),
        grid=(N_TILES,),
        in_specs=in_specs,
        out_specs=out_specs,
        out_shape=out_shape,
        scratch_shapes=[pltpu.VMEM((TM, D_MODEL), BF16), pltpu.VMEM((TM, FFN_DIM), BF16)],
        compiler_params=_params(("arbitrary",)),
        name="ffn_sublayer",
    )(*args)


def _rope(t, table):
    w = t * table
    return w + pltpu.roll(w, ROPE_DIM, axis=1)


def _mla_proj_kernel(x_ref, modp_ref, mods_ref, g_ref, win_ref, qn_ref, wuq_ref, kvn_ref,
                     wukt_ref, wuk_ref, wuvt_ref, tabp_ref, tabs_ref,
                     latp_ref, krp_ref, lats_ref, krs_ref, qp_ref, kp_ref, vt_ref, qs_ref):
    is_p = pl.program_id(0) < N_PTILES
    h = _modulated(x_ref[...], g_ref, modp_ref, mods_ref)
    z = _dot(h, win_ref[...])
    table = jnp.where(is_p, tabp_ref[...], tabs_ref[...])
    k_rope = _rope(z[:, Q_RANK + KV_RANK:], table)[:, :ROPE_DIM]
    latent = _rms(z[:, Q_RANK:Q_RANK + KV_RANK]) * kvn_ref[...]
    qn = (_rms(z[:, :Q_RANK]) * qn_ref[...]).astype(BF16)
    q_all = _dot(qn, wuq_ref[...])

    def q_rope(hd):
        t = q_all[:, hd * Q_HEAD_COLS + NOPE_DIM:(hd + 1) * Q_HEAD_COLS]
        return (_rope(t, table)[:, :ROPE_DIM] * ATTN_SCALE).astype(BF16)

    @pl.when(is_p)
    def _():
        latp_ref[...] = latent
        krp_ref[...] = k_rope
        lat_b = latent.astype(BF16)
        k_rope_b = k_rope.astype(BF16)
        k_nope = _dot(lat_b, wuk_ref[...])
        v_t = _dot_nt(wuvt_ref[...], lat_b).astype(BF16)
        for kb in range(TM // TK):
            vt_ref[kb] = v_t[:, kb * TK:(kb + 1) * TK]
        for hd in range(N_HEADS):
            q_nope = q_all[:, hd * Q_HEAD_COLS:hd * Q_HEAD_COLS + NOPE_DIM]
            qp_ref[hd, :, :NOPE_DIM] = (q_nope * ATTN_SCALE).astype(BF16)
            qp_ref[hd, :, NOPE_DIM:] = q_rope(hd)
            kp_ref[hd, :, :NOPE_DIM] = k_nope[:, hd * NOPE_DIM:(hd + 1) * NOPE_DIM].astype(BF16)
            kp_ref[hd, :, NOPE_DIM:] = k_rope_b

    @pl.when(jnp.logical_not(is_p))
    def _():
        lats_ref[...] = latent
        krs_ref[...] = k_rope
        for hd in range(N_HEADS):
            q_nope = q_all[:, hd * Q_HEAD_COLS:hd * Q_HEAD_COLS + NOPE_DIM].astype(BF16)
            q_lat = _dot(q_nope, wukt_ref[hd]) * ATTN_SCALE
            qs_ref[hd, :, :KV_RANK] = q_lat.astype(BF16)
            qs_ref[hd, :, KV_RANK:] = q_rope(hd)


def _mla_project(x, modp, mods, g, w_in_ext, q_norm, w_uq_ext, kv_norm, w_uk_t, w_uk_flat,
                 w_uv_t, table_p, table_s):
    return pl.pallas_call(
        _mla_proj_kernel,
        grid=(N_TILES,),
        in_specs=[_row_spec()] + _mod_specs() + [
            _resident((1, D_MODEL)),
            _resident(w_in_ext.shape),
            _resident((1, Q_RANK)),
            _resident(w_uq_ext.shape),
            _resident((1, KV_RANK)),
            _resident(w_uk_t.shape),
            _resident(w_uk_flat.shape),
            _resident(w_uv_t.shape),
            pl.BlockSpec((TM, 2 * ROPE_DIM), lambda i: (i % TILES_PER_SEQ, 0)),
            _resident((TM, 2 * ROPE_DIM)),
        ],
        out_specs=[
            _prow_spec(KV_RANK),
            _prow_spec(ROPE_DIM),
            _srow_spec(KV_RANK),
            _srow_spec(ROPE_DIM),
            pl.BlockSpec((N_HEADS, TM, HEAD_DIM), lambda i: (0, _ptile(i), 0)),
            pl.BlockSpec((N_HEADS, TM, HEAD_DIM), lambda i: (0, _ptile(i), 0)),
            pl.BlockSpec((TM // TK, N_HEADS * V_DIM, TK), lambda i: (_ptile(i), 0, 0)),
            pl.BlockSpec((N_HEADS, TM, QK_DIM), lambda i: (0, _stile(i), 0)),
        ],
        out_shape=[
            jax.ShapeDtypeStruct((P_ROWS, KV_RANK), F32),
            jax.ShapeDtypeStruct((P_ROWS, ROPE_DIM), F32),
            jax.ShapeDtypeStruct((S_ROWS, KV_RANK), F32),
            jax.ShapeDtypeStruct((S_ROWS, ROPE_DIM), F32),
            jax.ShapeDtypeStruct((N_HEADS, P_ROWS, HEAD_DIM), BF16),
            jax.ShapeDtypeStruct((N_HEADS, P_ROWS, HEAD_DIM), BF16),
            jax.ShapeDtypeStruct((P_ROWS // TK, N_HEADS * V_DIM, TK), BF16),
            jax.ShapeDtypeStruct((N_HEADS, S_ROWS, QK_DIM), BF16),
        ],
        compiler_params=_params(("arbitrary",)),
        name="mla_project",
    )(x, modp, mods, g, w_in_ext, q_norm, w_uq_ext, kv_norm, w_uk_t, w_uk_flat, w_uv_t,
      table_p, table_s)


def _attn_prompt_kernel(q_ref, k_ref, vt_ref, o_ref, m_s, l_s, acc_s, a_s, p_s):
    qi = pl.program_id(1)
    m_s[...] = jnp.full_like(m_s, -jnp.inf)
    l_s[...] = jnp.zeros_like(l_s)
    acc_s[...] = jnp.zeros_like(acc_s)

    def step(kb0, n_kb, masked):
        keys = n_kb * TK
        off = pl.multiple_of(kb0 * TK, TK)
        for hd in range(N_HEADS):
            s = _dot_nt(k_ref[hd, pl.ds(off, keys), :], q_ref[hd])
            if masked:
                k_pos = off + lax.broadcasted_iota(jnp.int32, (keys, TQ), 0)
                q_pos = qi * TQ + lax.broadcasted_iota(jnp.int32, (keys, TQ), 1)
                s = jnp.where(k_pos <= q_pos, s, -jnp.inf)
            m_old = m_s[hd]
            m_new = jnp.maximum(m_old, jnp.max(s, axis=0, keepdims=True))
            alpha = jnp.exp(m_old - m_new)
            p = jnp.exp(s - m_new)
            l_s[hd] = alpha * l_s[hd] + jnp.sum(p, axis=0, keepdims=True)
            p_s[hd, :keys] = p.astype(BF16)
            a_s[hd] = alpha
            m_s[hd] = m_new
        for hd in range(N_HEADS):
            pv = _dot(vt_ref[kb0, hd * V_DIM:(hd + 1) * V_DIM, :], p_s[hd, :TK])
            for j in range(1, n_kb):
                pv += _dot(vt_ref[kb0 + j, hd * V_DIM:(hd + 1) * V_DIM, :],
                           p_s[hd, j * TK:(j + 1) * TK])
            acc_s[hd] = a_s[hd] * acc_s[hd] + pv

    def body(j, carry):
        step(2 * j, 2, False)
        return carry

    lax.fori_loop(0, qi // 2, body, 0)

    @pl.when(qi % 2 == 1)
    def _():
        step(qi - 1, 1, False)

    step(qi, 1, True)
    for hd in range(N_HEADS):
        o_t = acc_s[hd] / l_s[hd]
        o_ref[:, hd * V_DIM:(hd + 1) * V_DIM] = o_t.T.astype(BF16)


def _attn_prompt(q_p, k_p, v_t):
    nq = SEQ // TQ
    return pl.pallas_call(
        _attn_prompt_kernel,
        grid=(BATCH, nq),
        in_specs=[
            pl.BlockSpec((N_HEADS, TQ, HEAD_DIM), lambda b, i: (0, b * nq + i, 0)),
            pl.BlockSpec((N_HEADS, SEQ, HEAD_DIM), lambda b, i: (0, b, 0)),
            pl.BlockSpec((N_KBLOCKS, N_HEADS * V_DIM, TK), lambda b, i: (b, 0, 0)),
        ],
        out_specs=pl.BlockSpec((TQ, N_HEADS * V_DIM), lambda b, i: (b * nq + i, 0)),
        out_shape=jax.ShapeDtypeStruct((P_ROWS, N_HEADS * V_DIM), BF16),
        scratch_shapes=[pltpu.VMEM((N_HEADS, 1, TQ), F32), pltpu.VMEM((N_HEADS, 1, TQ), F32),
                        pltpu.VMEM((N_HEADS, V_DIM, TQ), F32), pltpu.VMEM((N_HEADS, 1, TQ), F32),
                        pltpu.VMEM((N_HEADS, 2 * TK, TQ), BF16)],
        compiler_params=_params(("arbitrary", "arbitrary")),
        name="attn_prompt",
    )(q_p, k_p, v_t)


def _attn_sample_kernel(layer, pt_ref, q_ref, ln_ref, rn_ref, lat_hbm, rope_hbm, o_ref,
                        lat_buf, rope_buf, kl_s, kr_s, sem):
    b = pl.program_id(0)
    slot = lax.rem(b, 2)
    rows = N_HEADS * DEC_SEQ

    def page_copies(page, sl, p):
        return (pltpu.make_async_copy(lat_hbm.at[layer, page], lat_buf.at[sl, p], sem.at[sl, 0]),
                pltpu.make_async_copy(rope_hbm.at[layer, page], rope_buf.at[sl, p], sem.at[sl, 1]))

    def start_pages(seq, sl):
        def body(p, carry):
            for cp in page_copies(pt_ref[seq, p], sl, p):
                cp.start()
            return carry
        lax.fori_loop(0, N_PAGES, body, 0, unroll=8)

    @pl.when(b == 0)
    def _():
        start_pages(0, 0)

    @pl.when(b + 1 < DEC_BATCH)
    def _():
        start_pages(b + 1, 1 - slot)

    for p in range(N_PAGES):
        for cp in page_copies(0, slot, p):
            cp.wait()

    for p in range(N_PAGES):
        kl_s[p * PAGE_SIZE:(p + 1) * PAGE_SIZE, :] = lat_buf[slot, p].astype(BF16)
        kr_s[:, p * PAGE_SIZE:(p + 1) * PAGE_SIZE] = rope_buf[slot, p].astype(BF16)

    q = q_ref[0]
    q_lat = q[:, :KV_RANK]
    q_rope = q[:, KV_RANK:]
    k_lat = kl_s[...]
    s = _dot_nt(q_lat, k_lat) + _dot(q_rope, kr_s[...])
    new_lat = ln_ref[...].astype(BF16)
    new_rope = rn_ref[...].astype(BF16)
    s_new = _dot_nt(q_lat, new_lat) + _dot_nt(q_rope, new_rope)
    q_pos = lax.broadcasted_iota(jnp.int32, (rows, DEC_SEQ), 0) & (DEC_SEQ - 1)
    k_pos = lax.broadcasted_iota(jnp.int32, (rows, DEC_SEQ), 1)
    s_new = jnp.where(k_pos <= q_pos, s_new, -jnp.inf)
    m = jnp.maximum(jnp.max(s, axis=-1, keepdims=True), jnp.max(s_new, axis=-1, keepdims=True))
    p = jnp.exp(s - m)
    p_new = jnp.exp(s_new - m)
    l = jnp.sum(p, axis=-1, keepdims=True) + jnp.sum(p_new, axis=-1, keepdims=True)
    ctx = (_dot(p.astype(BF16), k_lat) + _dot(p_new.astype(BF16), new_lat)) / l
    for hd in range(N_HEADS):
        o_ref[0, :, hd * KV_RANK:(hd + 1) * KV_RANK] = ctx[hd * DEC_SEQ:(hd + 1) * DEC_SEQ]


def _attn_sample(layer, page_table, q_s, lat, k_rope, cache_lat, cache_rope_t):
    rows = N_HEADS * DEC_SEQ
    return pl.pallas_call(
        functools.partial(_attn_sample_kernel, layer),
        grid_spec=pltpu.PrefetchScalarGridSpec(
            num_scalar_prefetch=1,
            grid=(DEC_BATCH,),
            in_specs=[
                pl.BlockSpec((1, rows, QK_DIM), lambda b, pt: (b, 0, 0)),
                pl.BlockSpec((DEC_SEQ, KV_RANK), lambda b, pt: (b, 0)),
                pl.BlockSpec((DEC_SEQ, ROPE_DIM), lambda b, pt: (b, 0)),
                pl.BlockSpec(memory_space=pl.ANY),
                pl.BlockSpec(memory_space=pl.ANY),
            ],
            out_specs=pl.BlockSpec((1, DEC_SEQ, N_HEADS * KV_RANK), lambda b, pt: (b, 0, 0)),
            scratch_shapes=[
                pltpu.VMEM((2, N_PAGES, PAGE_SIZE, KV_RANK), F32),
                pltpu.VMEM((2, N_PAGES, ROPE_DIM, PAGE_SIZE), F32),
                pltpu.VMEM((PAST_LEN, KV_RANK), BF16),
                pltpu.VMEM((ROPE_DIM, PAST_LEN), BF16),
                pltpu.SemaphoreType.DMA((2, 2)),
            ],
        ),
        out_shape=jax.ShapeDtypeStruct((DEC_BATCH, DEC_SEQ, N_HEADS * KV_RANK), F32),
        compiler_params=_params(("arbitrary",)),
        name="attn_sample",
    )(page_table, q_s, lat, k_rope, cache_lat, cache_rope_t)


def _mla_out_kernel(x_ref, modp_ref, mods_ref, op_ref, cs_ref, wuv_ref, wo_ref, o_ref, o_s):
    is_p = pl.program_id(0) < N_PTILES

    @pl.when(is_p)
    def _():
        o_s[...] = op_ref[...]

    @pl.when(jnp.logical_not(is_p))
    def _():
        for hd in range(N_HEADS):
            ctx = cs_ref[:, hd * KV_RANK:(hd + 1) * KV_RANK].astype(BF16)
            o_s[:, hd * V_DIM:(hd + 1) * V_DIM] = _dot(ctx, wuv_ref[hd]).astype(BF16)

    y = _dot(o_s[...], wo_ref[...])
    o_ref[...] = x_ref[...] + _mod_vec(modp_ref, mods_ref, 2) * y


def _mla_output(x, modp, mods, o_p, ctx_s, w_uv_h, w_o):
    row = pl.BlockSpec((TM, D_MODEL), lambda i: (i, 0))
    return pl.pallas_call(
        _mla_out_kernel,
        grid=(N_TILES,),
        in_specs=[row] + _mod_specs() + [
            pl.BlockSpec((TM, N_HEADS * V_DIM), lambda i: (_ptile(i), 0)),
            pl.BlockSpec((TM, N_HEADS * KV_RANK), lambda i: (_stile(i), 0)),
            _resident(w_uv_h.shape),
            _resident(w_o.shape),
        ],
        out_specs=row,
        out_shape=jax.ShapeDtypeStruct((ROWS, D_MODEL), F32),
        scratch_shapes=[pltpu.VMEM((TM, N_HEADS * V_DIM), BF16)],
        compiler_params=_params(("arbitrary",)),
        name="mla_output",
    )(x, modp, mods, o_p, ctx_s, w_uv_h, w_o)


def _gmlp_kernel(x_ref, modp_ref, mods_ref, g_ref, win_ref, bin_ref, lng_ref, lnb_ref,
                 wmp_ref, wms_ref, bp_ref, bs_ref, wout_ref, o_ref, vp_ref, vs_ref, vb_s, t_s):
    i = pl.program_id(0)
    is_p = i < N_PTILES
    x = x_ref[...]
    h = _modulated(x, g_ref, modp_ref, mods_ref)
    z = jax.nn.gelu(_dot(h, win_ref[...]) + bin_ref[...])
    v = z[:, GMLP_DIM:]
    mu = jnp.mean(v, axis=-1, keepdims=True)
    var = jnp.mean(jnp.square(v - mu), axis=-1, keepdims=True)
    vn = (v - mu) * lax.rsqrt(var + EPS) * lng_ref[...] + lnb_ref[...]

    @pl.when(jnp.logical_and(is_p, i % TILES_PER_SEQ == TILES_PER_SEQ - 1))
    def _():
        vp_ref[...] = vn[TM - CHUNK:]

    @pl.when(jnp.logical_not(is_p))
    def _():
        vs_ref[...] = vn

    vb_s[...] = vn.astype(BF16)
    r = lax.broadcasted_iota(jnp.int32, (CHUNK, CHUNK), 0)
    c = lax.broadcasted_iota(jnp.int32, (CHUNK, CHUNK), 1)
    same_seq = jnp.logical_or(is_p, jnp.bitwise_xor(r, c) < DEC_SEQ)
    keep = jnp.logical_and(r >= c, same_seq)
    bias = jnp.where(is_p, bp_ref[...], bs_ref[...])
    for g in range(GMLP_GROUPS):
        cols = slice(g * GROUP_DIM, (g + 1) * GROUP_DIM)
        w = jnp.where(keep, jnp.where(is_p, wmp_ref[g], wms_ref[g]), 0.0).astype(BF16)
        for ch in range(TM // CHUNK):
            rws = slice(ch * CHUNK, (ch + 1) * CHUNK)
            mixed = _dot(w, vb_s[rws, cols]) + bias[:, cols]
            t_s[rws, cols] = (z[rws, cols] * mixed).astype(BF16)
    y = _dot(t_s[...], wout_ref[...])
    o_ref[...] = x + _mod_vec(modp_ref, mods_ref, 2) * y


def _gmlp_sublayer(x, modp, mods, g, w_in, b_in, ln_g, ln_b, wm_p, wm_s, bias_p, bias_s, w_out):
    row = pl.BlockSpec((TM, D_MODEL), lambda i: (i, 0))
    return pl.pallas_call(
        _gmlp_kernel,
        grid=(N_TILES,),
        in_specs=[row] + _mod_specs() + [
            _resident((1, D_MODEL)),
            _resident(w_in.shape),
            _resident((1, 2 * GMLP_DIM)),
            _resident((1, GMLP_DIM)),
            _resident((1, GMLP_DIM)),
            _resident(wm_p.shape),
            _resident(wm_s.shape),
            _resident(bias_p.shape),
            _resident(bias_s.shape),
            _resident(w_out.shape),
        ],
        out_specs=[
            row,
            pl.BlockSpec((CHUNK, GMLP_DIM),
                         lambda i: (jnp.minimum(i // TILES_PER_SEQ, BATCH - 1), 0)),
            _srow_spec(GMLP_DIM),
        ],
        out_shape=[jax.ShapeDtypeStruct((ROWS, D_MODEL), F32),
                   jax.ShapeDtypeStruct((BATCH * CHUNK, GMLP_DIM), F32),
                   jax.ShapeDtypeStruct((S_ROWS, GMLP_DIM), F32)],
        scratch_shapes=[pltpu.VMEM((TM, GMLP_DIM), BF16), pltpu.VMEM((TM, GMLP_DIM), BF16)],
        compiler_params=_params(("arbitrary",)),
        name="gmlp_sublayer",
    )(x, modp, mods, g, w_in, b_in, ln_g, ln_b, wm_p, wm_s, bias_p, bias_s, w_out)


def _rope_table(pos):
    half = ROPE_DIM // 2
    freqs = jnp.power(ROPE_THETA, -jnp.arange(half, dtype=F32) / half)
    ang = pos[:, None] * freqs[None, :]
    cos, sin = jnp.cos(ang), jnp.sin(ang)
    return jnp.concatenate([cos, cos, -sin, sin], axis=-1)


def _with_rotated(w):
    half = ROPE_DIM // 2
    return jnp.concatenate([w, w[..., -half:], w[..., -ROPE_DIM:-half]], axis=-1)


def _split_mods(mod_l):
    out = []
    for k in range(N_SUB):
        modp = mod_l[:BATCH, k]
        mods = jnp.repeat(mod_l[BATCH:, k], DEC_SEQ, axis=0).transpose(1, 0, 2)
        out.append((modp, mods))
    return out


def kernel(x_prompt, x_sample, cache_kv_latent, cache_k_rope, page_table, c_prompt, c_sample,
           ada_w, ada_b, norm_g, ffn_w_in, ffn_w_out,
           a_w_in, a_q_norm, a_w_uq, a_kv_norm, a_w_uk, a_w_uv, a_w_o,
           b_w_in, b_b_in, b_ln_g, b_ln_b, b_w_s, b_b_s, b_w_out, final_g):
    xs = (x_prompt.reshape(P_ROWS, D_MODEL), x_sample.reshape(S_ROWS, D_MODEL))
    c_all = jnp.concatenate([c_prompt, c_sample])
    mod = _ada_modulation(c_all, ada_w, ada_b).reshape(DEPTH, BATCH + DEC_BATCH, N_SUB, 3, D_MODEL)
    table_p = _rope_table(jnp.arange(SEQ, dtype=F32))
    table_s = _rope_table(jnp.tile(jnp.arange(DEC_SEQ, dtype=F32) + PAST_LEN, TM // DEC_SEQ))
    w_in = ffn_w_in.astype(BF16)
    w_out = ffn_w_out.astype(BF16)
    cache_rope_t = cache_k_rope.transpose(0, 1, 3, 2)

    lat_p, rope_p, lat_s, rope_s, v_p, v_s = [], [], [], [], [], []
    for i in range(DEPTH):
        mods_i = _split_mods(mod[i])
        j = i // 2
        x = _ffn_sublayer(xs if i == 0 else (x,), *mods_i[0], norm_g[i, 0][None],
                          w_in[i, 0], w_out[i, 0])
        if i % 2 == 0:
            w_in_ext = _with_rotated(a_w_in[j]).astype(BF16)
            w_uq_ext = _with_rotated(a_w_uq[j]).reshape(Q_RANK, N_HEADS * Q_HEAD_COLS).astype(BF16)
            w_uk_t = a_w_uk[j].transpose(1, 2, 0).astype(BF16)
            w_uk_flat = a_w_uk[j].reshape(KV_RANK, N_HEADS * NOPE_DIM).astype(BF16)
            w_uv_h = a_w_uv[j].transpose(1, 0, 2).astype(BF16)
            w_uv_t = a_w_uv[j].reshape(KV_RANK, N_HEADS * V_DIM).T.astype(BF16)
            l_p, r_p, l_s, r_s, q_p, k_p, v_t, q_s = _mla_project(
                x, *mods_i[1], norm_g[i, 1][None], w_in_ext, a_q_norm[j][None], w_uq_ext,
                a_kv_norm[j][None], w_uk_t, w_uk_flat, w_uv_t, table_p, table_s)
            o_p = _attn_prompt(q_p, k_p, v_t)
            q_s = q_s.reshape(N_HEADS, DEC_BATCH, DEC_SEQ, QK_DIM)
            q_s = q_s.transpose(1, 0, 2, 3).reshape(DEC_BATCH, N_HEADS * DEC_SEQ, QK_DIM)
            ctx_s = _attn_sample(j, page_table, q_s, l_s, r_s, cache_kv_latent, cache_rope_t)
            x = _mla_output(x, *mods_i[1], o_p, ctx_s.reshape(S_ROWS, N_HEADS * KV_RANK),
                            w_uv_h, a_w_o[j].astype(BF16))
            lat_p.append(l_p.reshape(BATCH, SEQ, KV_RANK))
            rope_p.append(r_p.reshape(BATCH, SEQ, ROPE_DIM))
            lat_s.append(l_s.reshape(DEC_BATCH, DEC_SEQ, KV_RANK))
            rope_s.append(r_s.reshape(DEC_BATCH, DEC_SEQ, ROPE_DIM))
        else:
            wm_s = jnp.tile(b_w_s[j][:, :DEC_SEQ, :DEC_SEQ], (1, CHUNK // DEC_SEQ, CHUNK // DEC_SEQ))
            bias_p = jnp.repeat(b_b_s[j].T, GROUP_DIM, axis=1)
            bias_s = jnp.tile(bias_p[:DEC_SEQ], (CHUNK // DEC_SEQ, 1))
            x, vp, vs = _gmlp_sublayer(
                x, *mods_i[1], norm_g[i, 1][None], b_w_in[j].astype(BF16), b_b_in[j][None],
                b_ln_g[j][None], b_ln_b[j][None], b_w_s[j], wm_s, bias_p, bias_s,
                b_w_out[j].astype(BF16))
            v_p.append(vp.reshape(BATCH, CHUNK, GMLP_DIM))
            v_s.append(vs.reshape(DEC_BATCH, DEC_SEQ, GMLP_DIM))
        x = _ffn_sublayer((x,), *mods_i[2], norm_g[i, 2][None], w_in[i, 1], w_out[i, 1],
                          final_g=final_g[None] if i == DEPTH - 1 else None)

    y_p, y_s = x
    return (
        y_p.reshape(BATCH, SEQ, D_MODEL),
        y_s.reshape(DEC_BATCH, DEC_SEQ, D_MODEL),
        jnp.stack(lat_p), jnp.stack(rope_p), jnp.stack(lat_s), jnp.stack(rope_s),
        jnp.stack(v_p), jnp.stack(v_s),
    )
```

```python
import functools

import jax
import jax.numpy as jnp
from jax import lax
from jax.experimental import pallas as pl
from jax.experimental.pallas import tpu as pltpu

D_MODEL = 1024
BATCH = 8
SEQ = 2048
DEPTH = 4
DEC_BATCH = 128
DEC_SEQ = 8
PAST_LEN = 8192
PAGE_SIZE = 128
N_PAGES = PAST_LEN // PAGE_SIZE
N_SUB = 3
HALF_STEP = 0.5
EPS = 1e-6
N_HEADS = 8
NOPE_DIM = 128
ROPE_DIM = 64
V_DIM = 128
Q_RANK = 384
KV_RANK = 256
ROPE_THETA = 10000.0
ATTN_SCALE = (NOPE_DIM + ROPE_DIM) ** -0.5
CHUNK = 128
GMLP_DIM = D_MODEL
GMLP_GROUPS = 8
GROUP_DIM = GMLP_DIM // GMLP_GROUPS
FFN_DIM = 2816

P_ROWS = BATCH * SEQ
S_ROWS = DEC_BATCH * DEC_SEQ
ROWS = P_ROWS + S_ROWS
TM = 512
N_TILES = ROWS // TM
TILE_SEQS = TM // DEC_SEQ
N_PTILES = P_ROWS // TM
TILES_PER_SEQ = SEQ // TM
FFN_CHUNK = 256
N_FFN_CHUNKS = FFN_DIM // FFN_CHUNK
QK_DIM = KV_RANK + ROPE_DIM
HEAD_DIM = NOPE_DIM + ROPE_DIM
Q_HEAD_COLS = 2 * NOPE_DIM
TQ = 256
TK = 256
N_KBLOCKS = SEQ // TK
VMEM_LIMIT = 56 * 1024 * 1024

F32 = jnp.float32
BF16 = jnp.bfloat16


def _params(sem):
    return pltpu.CompilerParams(dimension_semantics=sem, vmem_limit_bytes=VMEM_LIMIT)


def _resident(shape):
    nd = len(shape)
    return pl.BlockSpec(shape, lambda *_: (0,) * nd, pipeline_mode=pl.Buffered(1))


def _ptile(i):
    return jnp.minimum(i, N_PTILES - 1)


def _stile(i):
    return jnp.maximum(i - N_PTILES, 0)


def _mod_specs():
    modp = _resident((3, BATCH, D_MODEL))
    mods = pl.BlockSpec((3, TILE_SEQS, 1, D_MODEL), lambda i: (0, _stile(i), 0, 0))
    return [modp, mods]


def _mod_vec(modp_ref, mods_ref, j):
    i = pl.program_id(0)
    b = jnp.minimum(i // TILES_PER_SEQ, BATCH - 1)
    return jnp.where(i < N_PTILES, modp_ref[j, pl.ds(b, 1), :][None], mods_ref[j])


def _per_seq(a):
    return a.reshape(TILE_SEQS, DEC_SEQ, a.shape[-1])


def _rms(x):
    return x * lax.rsqrt(jnp.mean(x * x, axis=-1, keepdims=True) + EPS)


def _modulated(x, g_ref, modp_ref, mods_ref):
    shift = _mod_vec(modp_ref, mods_ref, 0)
    scale = _mod_vec(modp_ref, mods_ref, 1)
    y = _per_seq(_rms(x) * g_ref[...]) * (1 + scale) + shift
    return y.reshape(x.shape).astype(BF16)


def _gated(x, gate, y):
    return (_per_seq(x) + gate * _per_seq(y)).reshape(x.shape)


def _dot(a, b):
    return jnp.dot(a, b, preferred_element_type=F32)


def _dot_nt(a, b):
    return lax.dot_general(a, b, (((1,), (1,)), ((), ())), preferred_element_type=F32)


def _mod_kernel(c_ref, w_ref, b_ref, o_ref):
    c = c_ref[...]
    a = (c * jax.nn.sigmoid(c)).astype(BF16)
    o_ref[0, 0] = _dot(a, w_ref[0].astype(BF16)) + b_ref[0]


def _ada_modulation(c_all, ada_w, ada_b):
    n = c_all.shape[0]
    width = N_SUB * 3 * D_MODEL
    return pl.pallas_call(
        _mod_kernel,
        grid=(DEPTH, N_SUB * 3),
        in_specs=[
            pl.BlockSpec((n, D_MODEL), lambda l, j: (0, 0)),
            pl.BlockSpec((1, D_MODEL, D_MODEL), lambda l, j: (l, 0, j)),
            pl.BlockSpec((1, 1, D_MODEL), lambda l, j: (l, 0, j)),
        ],
        out_specs=pl.BlockSpec((1, 1, n, D_MODEL), lambda l, j: (l, j, 0, 0)),
        out_shape=jax.ShapeDtypeStruct((DEPTH, N_SUB * 3, n, D_MODEL), F32),
        compiler_params=_params(("arbitrary", "arbitrary")),
        name="ada_modulation",
    )(c_all, ada_w, ada_b.reshape(DEPTH, 1, width))


def _ffn_kernel(n_x, final, *refs):
    x_refs, refs = refs[:n_x], refs[n_x:]
    modp_ref, mods_ref, g_ref, win_ref, wo_ref = refs[:5]
    refs = refs[5:]
    is_p = pl.program_id(0) < N_PTILES
    x = x_refs[0][...] if n_x == 1 else jnp.where(is_p, x_refs[0][...], x_refs[1][...])
    h_s, a_s = refs[-2:]
    h_s[...] = _modulated(x, g_ref, modp_ref, mods_ref)
    for c in range(N_FFN_CHUNKS):
        h = h_s[...]
        gate = _dot(h, win_ref[0, 0, :, c * FFN_CHUNK:(c + 1) * FFN_CHUNK])
        up = _dot(h, win_ref[0, 0, :, FFN_DIM + c * FFN_CHUNK:FFN_DIM + (c + 1) * FFN_CHUNK])
        a_s[:, c * FFN_CHUNK:(c + 1) * FFN_CHUNK] = (gate * jax.nn.sigmoid(gate) * up).astype(BF16)
    y = _dot(a_s[...], wo_ref[0, 0])
    out = _gated(x, HALF_STEP * _mod_vec(modp_ref, mods_ref, 2), y)
    if not final:
        refs[0][...] = out
        return
    fg_ref, yp_ref, ys_ref = refs[:3]
    normed = _rms(out) * fg_ref[...]

    @pl.when(is_p)
    def _():
        yp_ref[...] = normed

    @pl.when(jnp.logical_not(is_p))
    def _():
        ys_ref[...] = normed


def _row_spec(width=D_MODEL):
    return pl.BlockSpec((TM, width), lambda i: (i, 0))


def _prow_spec(width=D_MODEL):
    return pl.BlockSpec((TM, width), lambda i: (_ptile(i), 0))


def _srow_spec(width=D_MODEL):
    return pl.BlockSpec((TM, width), lambda i: (_stile(i), 0))


def _ffn_sublayer(xs, modp, mods, g, w_in, w_out, layer, k, final_g=None):
    final = final_g is not None

    def weight(rows, cols):
        return pl.BlockSpec((1, 1, rows, cols), lambda i: (layer, k, 0, 0),
                            pipeline_mode=pl.Buffered(1))
    x_specs = [_row_spec()] if len(xs) == 1 else [_prow_spec(), _srow_spec()]
    in_specs = x_specs + _mod_specs() + [
        _resident((1, D_MODEL)),
        weight(D_MODEL, 2 * FFN_DIM),
        weight(FFN_DIM, D_MODEL),
    ]
    args = list(xs) + [modp, mods, g, w_in, w_out]
    if final:
        in_specs.append(_resident((1, D_MODEL)))
        args.append(final_g)
        out_specs = [_prow_spec(), _srow_spec()]
        out_shape = [jax.ShapeDtypeStruct((P_ROWS, D_MODEL), F32),
                     jax.ShapeDtypeStruct((S_ROWS, D_MODEL), F32)]
    else:
        out_specs = _row_spec()
        out_shape = jax.ShapeDtypeStruct((ROWS, D_MODEL), F32)
    return pl.pallas_call(
        functools.partial(_ffn_kernel, len(xs), final),
        grid=(N_TILES,),
        in_specs=in_specs,
        out_specs=out_specs,
        out_shape=out_shape,
        scratch_shapes=[pltpu.VMEM((TM, D_MODEL), BF16), pltpu.VMEM((TM, FFN_DIM), BF16)],
        compiler_params=_params(("arbitrary",)),
        name="ffn_sublayer",
    )(*args)


def _rope(t, table):
    w = t * table
    return w + pltpu.roll(w, ROPE_DIM, axis=1)


def _mla_proj_kernel(x_ref, modp_ref, mods_ref, g_ref, win_ref, qn_ref, wuq_ref, kvn_ref,
                     wukt_ref, wuk_ref, wuvt_ref, tabp_ref, tabs_ref,
                     latp_ref, krp_ref, lats_ref, krs_ref, qp_ref, kp_ref, vt_ref, qs_ref):
    is_p = pl.program_id(0) < N_PTILES
    h = _modulated(x_ref[...], g_ref, modp_ref, mods_ref)
    z = _dot(h, win_ref[...])
    table = jnp.where(is_p, tabp_ref[...], tabs_ref[...])
    k_rope = _rope(z[:, Q_RANK + KV_RANK:], table)[:, :ROPE_DIM]
    latent = _rms(z[:, Q_RANK:Q_RANK + KV_RANK]) * kvn_ref[...]
    qn = (_rms(z[:, :Q_RANK]) * qn_ref[...]).astype(BF16)
    q_all = _dot(qn, wuq_ref[...])

    def q_rope(hd):
        t = q_all[:, hd * Q_HEAD_COLS + NOPE_DIM:(hd + 1) * Q_HEAD_COLS]
        return (_rope(t, table)[:, :ROPE_DIM] * ATTN_SCALE).astype(BF16)

    @pl.when(is_p)
    def _():
        latp_ref[...] = latent
        krp_ref[...] = k_rope
        lat_b = latent.astype(BF16)
        k_rope_b = k_rope.astype(BF16)
        k_nope = _dot(lat_b, wuk_ref[...])
        v_t = _dot_nt(wuvt_ref[...], lat_b).astype(BF16)
        for kb in range(TM // TK):
            vt_ref[kb] = v_t[:, kb * TK:(kb + 1) * TK]
        for hd in range(N_HEADS):
            q_nope = q_all[:, hd * Q_HEAD_COLS:hd * Q_HEAD_COLS + NOPE_DIM]
            qp_ref[hd, :, :NOPE_DIM] = (q_nope * ATTN_SCALE).astype(BF16)
            qp_ref[hd, :, NOPE_DIM:] = q_rope(hd)
            kp_ref[hd, :, :NOPE_DIM] = k_nope[:, hd * NOPE_DIM:(hd + 1) * NOPE_DIM].astype(BF16)
            kp_ref[hd, :, NOPE_DIM:] = k_rope_b

    @pl.when(jnp.logical_not(is_p))
    def _():
        lats_ref[...] = latent
        krs_ref[...] = k_rope
        for hd in range(N_HEADS):
            q_nope = q_all[:, hd * Q_HEAD_COLS:hd * Q_HEAD_COLS + NOPE_DIM].astype(BF16)
            q_lat = _dot(q_nope, wukt_ref[hd]) * ATTN_SCALE
            qs_ref[hd, :, :KV_RANK] = q_lat.astype(BF16)
            qs_ref[hd, :, KV_RANK:] = q_rope(hd)


def _mla_project(x, modp, mods, g, w_in_ext, q_norm, w_uq_ext, kv_norm, w_uk_t, w_uk_flat,
                 w_uv_t, table_p, table_s):
    return pl.pallas_call(
        _mla_proj_kernel,
        grid=(N_TILES,),
        in_specs=[_row_spec()] + _mod_specs() + [
            _resident((1, D_MODEL)),
            _resident(w_in_ext.shape),
            _resident((1, Q_RANK)),
            _resident(w_uq_ext.shape),
            _resident((1, KV_RANK)),
            _resident(w_uk_t.shape),
            _resident(w_uk_flat.shape),
            _resident(w_uv_t.shape),
            pl.BlockSpec((TM, 2 * ROPE_DIM), lambda i: (i % TILES_PER_SEQ, 0)),
            _resident((TM, 2 * ROPE_DIM)),
        ],
        out_specs=[
            _prow_spec(KV_RANK),
            _prow_spec(ROPE_DIM),
            _srow_spec(KV_RANK),
            _srow_spec(ROPE_DIM),
            pl.BlockSpec((N_HEADS, TM, HEAD_DIM), lambda i: (0, _ptile(i), 0)),
            pl.BlockSpec((N_HEADS, TM, HEAD_DIM), lambda i: (0, _ptile(i), 0)),
            pl.BlockSpec((TM // TK, N_HEADS * V_DIM, TK), lambda i: (_ptile(i), 0, 0)),
            pl.BlockSpec((N_HEADS, TM, QK_DIM), lambda i: (0, _stile(i), 0)),
        ],
        out_shape=[
            jax.ShapeDtypeStruct((P_ROWS, KV_RANK), F32),
            jax.ShapeDtypeStruct((P_ROWS, ROPE_DIM), F32),
            jax.ShapeDtypeStruct((S_ROWS, KV_RANK), F32),
            jax.ShapeDtypeStruct((S_ROWS, ROPE_DIM), F32),
            jax.ShapeDtypeStruct((N_HEADS, P_ROWS, HEAD_DIM), BF16),
            jax.ShapeDtypeStruct((N_HEADS, P_ROWS, HEAD_DIM), BF16),
            jax.ShapeDtypeStruct((P_ROWS // TK, N_HEADS * V_DIM, TK), BF16),
            jax.ShapeDtypeStruct((N_HEADS, S_ROWS, QK_DIM), BF16),
        ],
        compiler_params=_params(("arbitrary",)),
        name="mla_project",
    )(x, modp, mods, g, w_in_ext, q_norm, w_uq_ext, kv_norm, w_uk_t, w_uk_flat, w_uv_t,
      table_p, table_s)


def _attn_prompt_kernel(q_ref, k_ref, vt_ref, o_ref, m_s, l_s, acc_s, a_s, p_s):
    qi = pl.program_id(1)
    m_s[...] = jnp.full_like(m_s, -jnp.inf)
    l_s[...] = jnp.zeros_like(l_s)
    acc_s[...] = jnp.zeros_like(acc_s)

    def step(kb0, n_kb, masked):
        keys = n_kb * TK
        off = pl.multiple_of(kb0 * TK, TK)
        for hd in range(N_HEADS):
            s = _dot_nt(k_ref[hd, pl.ds(off, keys), :], q_ref[hd])
            if masked:
                k_pos = off + lax.broadcasted_iota(jnp.int32, (keys, TQ), 0)
                q_pos = qi * TQ + lax.broadcasted_iota(jnp.int32, (keys, TQ), 1)
                s = jnp.where(k_pos <= q_pos, s, -jnp.inf)
            m_old = m_s[hd]
            m_new = jnp.maximum(m_old, jnp.max(s, axis=0, keepdims=True))
            alpha = jnp.exp(m_old - m_new)
            p = jnp.exp(s - m_new)
            l_s[hd] = alpha * l_s[hd] + jnp.sum(p, axis=0, keepdims=True)
            p_s[hd, :keys] = p.astype(BF16)
            a_s[hd] = alpha
            m_s[hd] = m_new
        for hd in range(N_HEADS):
            pv = _dot(vt_ref[kb0, hd * V_DIM:(hd + 1) * V_DIM, :], p_s[hd, :TK])
            for j in range(1, n_kb):
                pv += _dot(vt_ref[kb0 + j, hd * V_DIM:(hd + 1) * V_DIM, :],
                           p_s[hd, j * TK:(j + 1) * TK])
            acc_s[hd] = a_s[hd] * acc_s[hd] + pv

    def body(j, carry):
        step(2 * j, 2, False)
        return carry

    lax.fori_loop(0, qi // 2, body, 0)

    @pl.when(qi % 2 == 1)
    def _():
        step(qi - 1, 1, False)

    step(qi, 1, True)
    for hd in range(N_HEADS):
        o_t = acc_s[hd] / l_s[hd]
        o_ref[:, hd * V_DIM:(hd + 1) * V_DIM] = o_t.T.astype(BF16)


def _attn_prompt(q_p, k_p, v_t):
    nq = SEQ // TQ
    return pl.pallas_call(
        _attn_prompt_kernel,
        grid=(BATCH, nq),
        in_specs=[
            pl.BlockSpec((N_HEADS, TQ, HEAD_DIM), lambda b, i: (0, b * nq + i, 0)),
            pl.BlockSpec((N_HEADS, SEQ, HEAD_DIM), lambda b, i: (0, b, 0)),
            pl.BlockSpec((N_KBLOCKS, N_HEADS * V_DIM, TK), lambda b, i: (b, 0, 0)),
        ],
        out_specs=pl.BlockSpec((TQ, N_HEADS * V_DIM), lambda b, i: (b * nq + i, 0)),
        out_shape=jax.ShapeDtypeStruct((P_ROWS, N_HEADS * V_DIM), BF16),
        scratch_shapes=[pltpu.VMEM((N_HEADS, 1, TQ), F32), pltpu.VMEM((N_HEADS, 1, TQ), F32),
                        pltpu.VMEM((N_HEADS, V_DIM, TQ), F32), pltpu.VMEM((N_HEADS, 1, TQ), F32),
                        pltpu.VMEM((N_HEADS, 2 * TK, TQ), BF16)],
        compiler_params=_params(("arbitrary", "arbitrary")),
        name="attn_prompt",
    )(q_p, k_p, v_t)


def _attn_sample_kernel(layer, pt_ref, q_ref, ln_ref, rn_ref, lat_hbm, rope_hbm, o_ref,
                        lat_buf, rope_buf, kl_s, kr_s, sem):
    b = pl.program_id(0)
    slot = lax.rem(b, 2)
    rows = N_HEADS * DEC_SEQ

    def page_copies(page, sl, p):
        return (pltpu.make_async_copy(lat_hbm.at[layer, page], lat_buf.at[sl, p], sem.at[sl, 0]),
                pltpu.make_async_copy(rope_hbm.at[layer, page], rope_buf.at[sl, p], sem.at[sl, 1]))

    def start_pages(seq, sl):
        def body(p, carry):
            for cp in page_copies(pt_ref[seq, p], sl, p):
                cp.start()
            return carry
        lax.fori_loop(0, N_PAGES, body, 0, unroll=8)

    @pl.when(b == 0)
    def _():
        start_pages(0, 0)

    @pl.when(b + 1 < DEC_BATCH)
    def _():
        start_pages(b + 1, 1 - slot)

    for p in range(N_PAGES):
        for cp in page_copies(0, slot, p):
            cp.wait()

    for p in range(N_PAGES):
        kl_s[p * PAGE_SIZE:(p + 1) * PAGE_SIZE, :] = lat_buf[slot, p].astype(BF16)
        kr_s[:, p * PAGE_SIZE:(p + 1) * PAGE_SIZE] = rope_buf[slot, p].astype(BF16)

    q = q_ref[0]
    q_lat = q[:, :KV_RANK]
    q_rope = q[:, KV_RANK:]
    k_lat = kl_s[...]
    s = _dot_nt(q_lat, k_lat) + _dot(q_rope, kr_s[...])
    new_lat = ln_ref[...].astype(BF16)
    new_rope = rn_ref[...].astype(BF16)
    s_new = _dot_nt(q_lat, new_lat) + _dot_nt(q_rope, new_rope)
    q_pos = lax.broadcasted_iota(jnp.int32, (rows, DEC_SEQ), 0) & (DEC_SEQ - 1)
    k_pos = lax.broadcasted_iota(jnp.int32, (rows, DEC_SEQ), 1)
    s_new = jnp.where(k_pos <= q_pos, s_new, -jnp.inf)
    m = jnp.maximum(jnp.max(s, axis=-1, keepdims=True), jnp.max(s_new, axis=-1, keepdims=True))
    p = jnp.exp(s - m)
    p_new = jnp.exp(s_new - m)
    l = jnp.sum(p, axis=-1, keepdims=True) + jnp.sum(p_new, axis=-1, keepdims=True)
    pb = p.astype(BF16)
    half = KV_RANK // 2
    ctx = jnp.concatenate([_dot(pb, kl_s[:, :half]), _dot(pb, kl_s[:, half:])], axis=1)
    ctx = (ctx + _dot(p_new.astype(BF16), new_lat)) / l
    for hd in range(N_HEADS):
        o_ref[0, :, hd * KV_RANK:(hd + 1) * KV_RANK] = ctx[hd * DEC_SEQ:(hd + 1) * DEC_SEQ]


def _attn_sample(layer, page_table, q_s, lat, k_rope, cache_lat, cache_rope_t):
    rows = N_HEADS * DEC_SEQ
    return pl.pallas_call(
        functools.partial(_attn_sample_kernel, layer),
        grid_spec=pltpu.PrefetchScalarGridSpec(
            num_scalar_prefetch=1,
            grid=(DEC_BATCH,),
            in_specs=[
                pl.BlockSpec((1, rows, QK_DIM), lambda b, pt: (b, 0, 0)),
                pl.BlockSpec((DEC_SEQ, KV_RANK), lambda b, pt: (b, 0)),
                pl.BlockSpec((DEC_SEQ, ROPE_DIM), lambda b, pt: (b, 0)),
                pl.BlockSpec(memory_space=pl.ANY),
                pl.BlockSpec(memory_space=pl.ANY),
            ],
            out_specs=pl.BlockSpec((1, DEC_SEQ, N_HEADS * KV_RANK), lambda b, pt: (b, 0, 0)),
            scratch_shapes=[
                pltpu.VMEM((2, N_PAGES, PAGE_SIZE, KV_RANK), F32),
                pltpu.VMEM((2, N_PAGES, ROPE_DIM, PAGE_SIZE), F32),
                pltpu.VMEM((PAST_LEN, KV_RANK), BF16),
                pltpu.VMEM((ROPE_DIM, PAST_LEN), BF16),
                pltpu.SemaphoreType.DMA((2, 2)),
            ],
        ),
        out_shape=jax.ShapeDtypeStruct((DEC_BATCH, DEC_SEQ, N_HEADS * KV_RANK), F32),
        compiler_params=_params(("arbitrary",)),
        name="attn_sample",
    )(page_table, q_s, lat, k_rope, cache_lat, cache_rope_t)


def _mla_out_kernel(x_ref, modp_ref, mods_ref, op_ref, cs_ref, wuv_ref, wo_ref, o_ref, o_s):
    is_p = pl.program_id(0) < N_PTILES

    @pl.when(is_p)
    def _():
        o_s[...] = op_ref[...]

    @pl.when(jnp.logical_not(is_p))
    def _():
        for hd in range(N_HEADS):
            ctx = cs_ref[:, hd * KV_RANK:(hd + 1) * KV_RANK].astype(BF16)
            o_s[:, hd * V_DIM:(hd + 1) * V_DIM] = _dot(ctx, wuv_ref[hd]).astype(BF16)

    y = _dot(o_s[...], wo_ref[...])
    o_ref[...] = _gated(x_ref[...], _mod_vec(modp_ref, mods_ref, 2), y)


def _mla_output(x, modp, mods, o_p, ctx_s, w_uv_h, w_o):
    row = pl.BlockSpec((TM, D_MODEL), lambda i: (i, 0))
    return pl.pallas_call(
        _mla_out_kernel,
        grid=(N_TILES,),
        in_specs=[row] + _mod_specs() + [
            pl.BlockSpec((TM, N_HEADS * V_DIM), lambda i: (_ptile(i), 0)),
            pl.BlockSpec((TM, N_HEADS * KV_RANK), lambda i: (_stile(i), 0)),
            _resident(w_uv_h.shape),
            _resident(w_o.shape),
        ],
        out_specs=row,
        out_shape=jax.ShapeDtypeStruct((ROWS, D_MODEL), F32),
        scratch_shapes=[pltpu.VMEM((TM, N_HEADS * V_DIM), BF16)],
        compiler_params=_params(("arbitrary",)),
        name="mla_output",
    )(x, modp, mods, o_p, ctx_s, w_uv_h, w_o)


def _gmlp_kernel(x_ref, modp_ref, mods_ref, g_ref, win_ref, bin_ref, lng_ref, lnb_ref,
                 wmp_ref, wms_ref, bp_ref, bs_ref, wout_ref, o_ref, vp_ref, vs_ref, vb_s, t_s):
    i = pl.program_id(0)
    is_p = i < N_PTILES
    x = x_ref[...]
    h = _modulated(x, g_ref, modp_ref, mods_ref)
    z = jax.nn.gelu(_dot(h, win_ref[...]) + bin_ref[...])
    v = z[:, GMLP_DIM:]
    mu = jnp.mean(v, axis=-1, keepdims=True)
    var = jnp.mean(jnp.square(v - mu), axis=-1, keepdims=True)
    vn = (v - mu) * lax.rsqrt(var + EPS) * lng_ref[...] + lnb_ref[...]

    @pl.when(jnp.logical_and(is_p, i % TILES_PER_SEQ == TILES_PER_SEQ - 1))
    def _():
        vp_ref[...] = vn[TM - CHUNK:]

    @pl.when(jnp.logical_not(is_p))
    def _():
        vs_ref[...] = vn

    vb_s[...] = vn.astype(BF16)
    r = lax.broadcasted_iota(jnp.int32, (CHUNK, CHUNK), 0)
    c = lax.broadcasted_iota(jnp.int32, (CHUNK, CHUNK), 1)
    same_seq = jnp.logical_or(is_p, jnp.bitwise_xor(r, c) < DEC_SEQ)
    keep = jnp.logical_and(r >= c, same_seq)
    bias = jnp.where(is_p, bp_ref[...], bs_ref[...])
    for g in range(GMLP_GROUPS):
        cols = slice(g * GROUP_DIM, (g + 1) * GROUP_DIM)
        w = jnp.where(keep, jnp.where(is_p, wmp_ref[g], wms_ref[g]), 0.0).astype(BF16)
        for ch in range(TM // CHUNK):
            rws = slice(ch * CHUNK, (ch + 1) * CHUNK)
            mixed = _dot(w, vb_s[rws, cols]) + bias[:, cols]
            t_s[rws, cols] = (z[rws, cols] * mixed).astype(BF16)
    y = _dot(t_s[...], wout_ref[...])
    o_ref[...] = _gated(x, _mod_vec(modp_ref, mods_ref, 2), y)


def _gmlp_sublayer(x, modp, mods, g, w_in, b_in, ln_g, ln_b, wm_p, wm_s, bias_p, bias_s, w_out):
    row = pl.BlockSpec((TM, D_MODEL), lambda i: (i, 0))
    return pl.pallas_call(
        _gmlp_kernel,
        grid=(N_TILES,),
        in_specs=[row] + _mod_specs() + [
            _resident((1, D_MODEL)),
            _resident(w_in.shape),
            _resident((1, 2 * GMLP_DIM)),
            _resident((1, GMLP_DIM)),
            _resident((1, GMLP_DIM)),
            _resident(wm_p.shape),
            _resident(wm_s.shape),
            _resident(bias_p.shape),
            _resident(bias_s.shape),
            _resident(w_out.shape),
        ],
        out_specs=[
            row,
            pl.BlockSpec((CHUNK, GMLP_DIM),
                         lambda i: (jnp.minimum(i // TILES_PER_SEQ, BATCH - 1), 0)),
            _srow_spec(GMLP_DIM),
        ],
        out_shape=[jax.ShapeDtypeStruct((ROWS, D_MODEL), F32),
                   jax.ShapeDtypeStruct((BATCH * CHUNK, GMLP_DIM), F32),
                   jax.ShapeDtypeStruct((S_ROWS, GMLP_DIM), F32)],
        scratch_shapes=[pltpu.VMEM((TM, GMLP_DIM), BF16), pltpu.VMEM((TM, GMLP_DIM), BF16)],
        compiler_params=_params(("arbitrary",)),
        name="gmlp_sublayer",
    )(x, modp, mods, g, w_in, b_in, ln_g, ln_b, wm_p, wm_s, bias_p, bias_s, w_out)


def _rope_table(pos):
    half = ROPE_DIM // 2
    freqs = jnp.power(ROPE_THETA, -jnp.arange(half, dtype=F32) / half)
    ang = pos[:, None] * freqs[None, :]
    cos, sin = jnp.cos(ang), jnp.sin(ang)
    return jnp.concatenate([cos, cos, -sin, sin], axis=-1)


def _with_rotated(w):
    half = ROPE_DIM // 2
    return jnp.concatenate([w, w[..., -half:], w[..., -ROPE_DIM:-half]], axis=-1)


def _split_mods(mod_l):
    out = []
    for k in range(N_SUB):
        m = mod_l[3 * k:3 * k + 3]
        out.append((m[:, :BATCH], m[:, BATCH:, None, :]))
    return out


def kernel(x_prompt, x_sample, cache_kv_latent, cache_k_rope, page_table, c_prompt, c_sample,
           ada_w, ada_b, norm_g, ffn_w_in, ffn_w_out,
           a_w_in, a_q_norm, a_w_uq, a_kv_norm, a_w_uk, a_w_uv, a_w_o,
           b_w_in, b_b_in, b_ln_g, b_ln_b, b_w_s, b_b_s, b_w_out, final_g):
    xs = (x_prompt.reshape(P_ROWS, D_MODEL), x_sample.reshape(S_ROWS, D_MODEL))
    c_all = jnp.concatenate([c_prompt, c_sample])
    mod = _ada_modulation(c_all, ada_w, ada_b)
    table_p = _rope_table(jnp.arange(SEQ, dtype=F32))
    table_s = _rope_table(jnp.tile(jnp.arange(DEC_SEQ, dtype=F32) + PAST_LEN, TM // DEC_SEQ))
    w_in = ffn_w_in.astype(BF16)
    w_out = ffn_w_out.astype(BF16)
    cache_rope_t = cache_k_rope.transpose(0, 1, 3, 2)

    lat_p, rope_p, lat_s, rope_s, v_p, v_s = [], [], [], [], [], []
    for i in range(DEPTH):
        mods_i = _split_mods(mod[i])
        j = i // 2
        x = _ffn_sublayer(xs if i == 0 else (x,), *mods_i[0], norm_g[i, 0][None], w_in, w_out, i, 0)
        if i % 2 == 0:
            w_in_ext = _with_rotated(a_w_in[j]).astype(BF16)
            w_uq_ext = _with_rotated(a_w_uq[j]).reshape(Q_RANK, N_HEADS * Q_HEAD_COLS).astype(BF16)
            w_uk_t = a_w_uk[j].transpose(1, 2, 0).astype(BF16)
            w_uk_flat = a_w_uk[j].reshape(KV_RANK, N_HEADS * NOPE_DIM).astype(BF16)
            w_uv_h = a_w_uv[j].transpose(1, 0, 2).astype(BF16)
            w_uv_t = a_w_uv[j].reshape(KV_RANK, N_HEADS * V_DIM).T.astype(BF16)
            l_p, r_p, l_s, r_s, q_p, k_p, v_t, q_s = _mla_project(
                x, *mods_i[1], norm_g[i, 1][None], w_in_ext, a_q_norm[j][None], w_uq_ext,
                a_kv_norm[j][None], w_uk_t, w_uk_flat, w_uv_t, table_p, table_s)
            o_p = _attn_prompt(q_p, k_p, v_t)
            q_s = q_s.reshape(N_HEADS, DEC_BATCH, DEC_SEQ, QK_DIM)
            q_s = q_s.transpose(1, 0, 2, 3).reshape(DEC_BATCH, N_HEADS * DEC_SEQ, QK_DIM)
            ctx_s = _attn_sample(j, page_table, q_s, l_s, r_s, cache_kv_latent, cache_rope_t)
            x = _mla_output(x, *mods_i[1], o_p, ctx_s.reshape(S_ROWS, N_HEADS * KV_RANK),
                            w_uv_h, a_w_o[j].astype(BF16))
            lat_p.append(l_p.reshape(BATCH, SEQ, KV_RANK))
            rope_p.append(r_p.reshape(BATCH, SEQ, ROPE_DIM))
            lat_s.append(l_s.reshape(DEC_BATCH, DEC_SEQ, KV_RANK))
            rope_s.append(r_s.reshape(DEC_BATCH, DEC_SEQ, ROPE_DIM))
        else:
            wm_s = jnp.tile(b_w_s[j][:, :DEC_SEQ, :DEC_SEQ], (1, CHUNK // DEC_SEQ, CHUNK // DEC_SEQ))
            bias_p = jnp.repeat(b_b_s[j].T, GROUP_DIM, axis=1)
            bias_s = jnp.tile(bias_p[:DEC_SEQ], (CHUNK // DEC_SEQ, 1))
            x, vp, vs = _gmlp_sublayer(
                x, *mods_i[1], norm_g[i, 1][None], b_w_in[j].astype(BF16), b_b_in[j][None],
                b_ln_g[j][None], b_ln_b[j][None], b_w_s[j], wm_s, bias_p, bias_s,
                b_w_out[j].astype(BF16))
            v_p.append(vp.reshape(BATCH, CHUNK, GMLP_DIM))
            v_s.append(vs.reshape(DEC_BATCH, DEC_SEQ, GMLP_DIM))
        x = _ffn_sublayer((x,), *mods_i[2], norm_g[i, 2][None], w_in, w_out, i, 1,
                          final_g=final_g[None] if i == DEPTH - 1 else None)

    y_p, y_s = x
    return (
        y_p.reshape(BATCH, SEQ, D_MODEL),
        y_s.reshape(DEC_BATCH, DEC_SEQ, D_MODEL),
        jnp.stack(lat_p), jnp.stack(rope_p), jnp.stack(lat_s), jnp.stack(rope_s),
        jnp.stack(v_p), jnp.stack(v_s),
    )
```

```python
import functools

import jax
import jax.numpy as jnp
from jax import lax
from jax.experimental import pallas as pl
from jax.experimental.pallas import tpu as pltpu

D_MODEL = 1024
BATCH = 8
SEQ = 2048
DEPTH = 4
DEC_BATCH = 128
DEC_SEQ = 8
PAST_LEN = 8192
PAGE_SIZE = 128
N_PAGES = PAST_LEN // PAGE_SIZE
N_SUB = 3
HALF_STEP = 0.5
EPS = 1e-6
N_HEADS = 8
NOPE_DIM = 128
ROPE_DIM = 64
V_DIM = 128
Q_RANK = 384
KV_RANK = 256
ROPE_THETA = 10000.0
ATTN_SCALE = (NOPE_DIM + ROPE_DIM) ** -0.5
LOG2_E = 1.4426950408889634
GELU_A = -2.0 * 0.7978845608028654 * LOG2_E
GELU_B = GELU_A * 0.044715
Q_SCALE = ATTN_SCALE * LOG2_E
CHUNK = 128
GMLP_DIM = D_MODEL
GMLP_GROUPS = 8
GROUP_DIM = GMLP_DIM // GMLP_GROUPS
FFN_DIM = 2816

P_ROWS = BATCH * SEQ
S_ROWS = DEC_BATCH * DEC_SEQ
ROWS = P_ROWS + S_ROWS
TM = 512
N_TILES = ROWS // TM
TILE_SEQS = TM // DEC_SEQ
N_PTILES = P_ROWS // TM
TILES_PER_SEQ = SEQ // TM
FFN_CHUNK = 256
N_FFN_CHUNKS = FFN_DIM // FFN_CHUNK
N_W_CHUNKS = 11
W_IN_COLS = 2 * FFN_DIM // N_W_CHUNKS
W_OUT_ROWS = FFN_DIM // N_W_CHUNKS
QK_DIM = KV_RANK + ROPE_DIM
HEAD_DIM = NOPE_DIM + ROPE_DIM
Q_HEAD_COLS = 2 * NOPE_DIM
TQ = 256
TK = 256
N_KBLOCKS = SEQ // TK
VMEM_LIMIT = 56 * 1024 * 1024

F32 = jnp.float32
BF16 = jnp.bfloat16


def _params(sem):
    return pltpu.CompilerParams(dimension_semantics=sem, vmem_limit_bytes=VMEM_LIMIT)


def _resident(shape):
    nd = len(shape)
    return pl.BlockSpec(shape, lambda *_: (0,) * nd, pipeline_mode=pl.Buffered(1))


def _ptile(i):
    return jnp.minimum(i, N_PTILES - 1)


def _stile(i):
    return jnp.maximum(i - N_PTILES, 0)


def _mod_specs():
    modp = _resident((3, BATCH, D_MODEL))
    mods = pl.BlockSpec((3, TILE_SEQS, 1, D_MODEL), lambda i: (0, _stile(i), 0, 0))
    return [modp, mods]


def _mod_vec(modp_ref, mods_ref, j):
    i = pl.program_id(0)
    b = jnp.minimum(i // TILES_PER_SEQ, BATCH - 1)
    return jnp.where(i < N_PTILES, modp_ref[j, pl.ds(b, 1), :][None], mods_ref[j])


def _per_seq(a):
    return a.reshape(TILE_SEQS, DEC_SEQ, a.shape[-1])


def _rms(x):
    return x * lax.rsqrt(jnp.mean(x * x, axis=-1, keepdims=True) + EPS)


def _modulated(x, g_ref, modp_ref, mods_ref):
    shift = _mod_vec(modp_ref, mods_ref, 0)
    scale = _mod_vec(modp_ref, mods_ref, 1)
    y = _per_seq(_rms(x) * g_ref[...]) * (1 + scale) + shift
    return y.reshape(x.shape).astype(BF16)


def _gated(x, gate, y):
    return (_per_seq(x) + gate * _per_seq(y)).reshape(x.shape)


def _dot(a, b):
    return jnp.dot(a, b, preferred_element_type=F32)


def _dot_nt(a, b):
    return lax.dot_general(a, b, (((1,), (1,)), ((), ())), preferred_element_type=F32)


def _mod_kernel(c_ref, w_ref, b_ref, o_ref):
    c = c_ref[...]
    a = (c * jax.nn.sigmoid(c)).astype(BF16)
    o_ref[0, 0] = _dot(a, w_ref[0].astype(BF16)) + b_ref[0]


def _ada_modulation(c_all, ada_w, ada_b):
    n = c_all.shape[0]
    width = N_SUB * 3 * D_MODEL
    return pl.pallas_call(
        _mod_kernel,
        grid=(DEPTH, N_SUB * 3),
        in_specs=[
            pl.BlockSpec((n, D_MODEL), lambda l, j: (0, 0)),
            pl.BlockSpec((1, D_MODEL, D_MODEL), lambda l, j: (l, 0, j)),
            pl.BlockSpec((1, 1, D_MODEL), lambda l, j: (l, 0, j)),
        ],
        out_specs=pl.BlockSpec((1, 1, n, D_MODEL), lambda l, j: (l, j, 0, 0)),
        out_shape=jax.ShapeDtypeStruct((DEPTH, N_SUB * 3, n, D_MODEL), F32),
        compiler_params=_params(("arbitrary", "arbitrary")),
        name="ada_modulation",
    )(c_all, ada_w, ada_b.reshape(DEPTH, 1, width))


def _ffn_kernel(n_x, final, layer, k, *refs):
    x_refs, refs = refs[:n_x], refs[n_x:]
    modp_ref, mods_ref, g_ref, win_hbm, wo_hbm = refs[:5]
    refs = refs[5:]
    h_s, a_s, win_s, wo_s, stin_s, sto_s, sem = refs[-7:]
    is_p = pl.program_id(0) < N_PTILES

    @pl.when(pl.program_id(0) == 0)
    def _():
        def in_copy(c):
            return pltpu.make_async_copy(
                win_hbm.at[layer, k, :, pl.ds(c * W_IN_COLS, W_IN_COLS)],
                stin_s.at[c % 2], sem.at[0, c % 2])

        def out_copy(c):
            return pltpu.make_async_copy(
                wo_hbm.at[layer, k, pl.ds(c * W_OUT_ROWS, W_OUT_ROWS), :],
                sto_s.at[c % 2], sem.at[1, c % 2])

        in_copy(0).start()
        out_copy(0).start()
        for c in range(N_W_CHUNKS):
            if c + 1 < N_W_CHUNKS:
                in_copy(c + 1).start()
                out_copy(c + 1).start()
            in_copy(c).wait()
            win_s[:, c * W_IN_COLS:(c + 1) * W_IN_COLS] = stin_s[c % 2].astype(BF16)
            out_copy(c).wait()
            wo_s[c * W_OUT_ROWS:(c + 1) * W_OUT_ROWS, :] = sto_s[c % 2].astype(BF16)

    x = x_refs[0][...] if n_x == 1 else jnp.where(is_p, x_refs[0][...], x_refs[1][...])
    h_s[...] = _modulated(x, g_ref, modp_ref, mods_ref)
    for c in range(N_FFN_CHUNKS):
        h = h_s[...]
        gate = _dot(h, win_s[:, c * FFN_CHUNK:(c + 1) * FFN_CHUNK])
        up = _dot(h, win_s[:, FFN_DIM + c * FFN_CHUNK:FFN_DIM + (c + 1) * FFN_CHUNK])
        a_s[:, c * FFN_CHUNK:(c + 1) * FFN_CHUNK] = (gate * jax.nn.sigmoid(gate) * up).astype(BF16)
    y = _dot(a_s[...], wo_s[...])
    out = _gated(x, HALF_STEP * _mod_vec(modp_ref, mods_ref, 2), y)
    if not final:
        refs[0][...] = out
        return
    fg_ref, yp_ref, ys_ref = refs[:3]
    normed = _rms(out) * fg_ref[...]

    @pl.when(is_p)
    def _():
        yp_ref[...] = normed

    @pl.when(jnp.logical_not(is_p))
    def _():
        ys_ref[...] = normed


def _row_spec(width=D_MODEL):
    return pl.BlockSpec((TM, width), lambda i: (i, 0))


def _prow_spec(width=D_MODEL):
    return pl.BlockSpec((TM, width), lambda i: (_ptile(i), 0))


def _srow_spec(width=D_MODEL):
    return pl.BlockSpec((TM, width), lambda i: (_stile(i), 0))


def _ffn_sublayer(xs, modp, mods, g, w_in, w_out, layer, k, final_g=None):
    final = final_g is not None
    x_specs = [_row_spec()] if len(xs) == 1 else [_prow_spec(), _srow_spec()]
    in_specs = x_specs + _mod_specs() + [
        _resident((1, D_MODEL)),
        pl.BlockSpec(memory_space=pl.ANY),
        pl.BlockSpec(memory_space=pl.ANY),
    ]
    args = list(xs) + [modp, mods, g, w_in, w_out]
    if final:
        in_specs.append(_resident((1, D_MODEL)))
        args.append(final_g)
        out_specs = [_prow_spec(), _srow_spec()]
        out_shape = [jax.ShapeDtypeStruct((P_ROWS, D_MODEL), F32),
                     jax.ShapeDtypeStruct((S_ROWS, D_MODEL), F32)]
    else:
        out_specs = _row_spec()
        out_shape = jax.ShapeDtypeStruct((ROWS, D_MODEL), F32)
    return pl.pallas_call(
        functools.partial(_ffn_kernel, len(xs), final, layer, k),
        grid=(N_TILES,),
        in_specs=in_specs,
        out_specs=out_specs,
        out_shape=out_shape,
        scratch_shapes=[
            pltpu.VMEM((TM, D_MODEL), BF16), pltpu.VMEM((TM, FFN_DIM), BF16),
            pltpu.VMEM((D_MODEL, 2 * FFN_DIM), BF16), pltpu.VMEM((FFN_DIM, D_MODEL), BF16),
            pltpu.VMEM((2, D_MODEL, W_IN_COLS), F32), pltpu.VMEM((2, W_OUT_ROWS, D_MODEL), F32),
            pltpu.SemaphoreType.DMA((2, 2)),
        ],
        compiler_params=_params(("arbitrary",)),
        name="ffn_sublayer",
    )(*args)


def _rope(t, table):
    w = t * table
    return w + pltpu.roll(w, ROPE_DIM, axis=1)


def _mla_proj_kernel(x_ref, modp_ref, mods_ref, g_ref, win_ref, qn_ref, wuq_ref, kvn_ref,
                     wukt_ref, wuk_ref, wuvt_ref, tabp_ref, tabs_ref,
                     latp_ref, krp_ref, lats_ref, krs_ref, qp_ref, kp_ref, vt_ref, qs_ref):
    is_p = pl.program_id(0) < N_PTILES
    h = _modulated(x_ref[...], g_ref, modp_ref, mods_ref)
    z = _dot(h, win_ref[...])
    table = jnp.where(is_p, tabp_ref[...], tabs_ref[...])
    k_rope = _rope(z[:, Q_RANK + KV_RANK:], table)[:, :ROPE_DIM]
    latent = _rms(z[:, Q_RANK:Q_RANK + KV_RANK]) * kvn_ref[...]
    qn = (_rms(z[:, :Q_RANK]) * qn_ref[...]).astype(BF16)
    q_all = _dot(qn, wuq_ref[...])

    def q_rope(hd):
        t = q_all[:, hd * Q_HEAD_COLS + NOPE_DIM:(hd + 1) * Q_HEAD_COLS]
        return (_rope(t, table)[:, :ROPE_DIM] * Q_SCALE).astype(BF16)

    @pl.when(is_p)
    def _():
        latp_ref[...] = latent
        krp_ref[...] = k_rope
        lat_b = latent.astype(BF16)
        k_rope_b = k_rope.astype(BF16)
        k_nope = _dot(lat_b, wuk_ref[...])
        v_t = _dot_nt(wuvt_ref[...], lat_b).astype(BF16)
        for kb in range(TM // TK):
            vt_ref[kb] = v_t[:, kb * TK:(kb + 1) * TK]
        for hd in range(N_HEADS):
            q_nope = q_all[:, hd * Q_HEAD_COLS:hd * Q_HEAD_COLS + NOPE_DIM]
            qp_ref[hd, :, :NOPE_DIM] = (q_nope * Q_SCALE).astype(BF16)
            qp_ref[hd, :, NOPE_DIM:] = q_rope(hd)
            kp_ref[hd, :, :NOPE_DIM] = k_nope[:, hd * NOPE_DIM:(hd + 1) * NOPE_DIM].astype(BF16)
            kp_ref[hd, :, NOPE_DIM:] = k_rope_b

    @pl.when(jnp.logical_not(is_p))
    def _():
        lats_ref[...] = latent
        krs_ref[...] = k_rope
        for hd in range(N_HEADS):
            q_nope = q_all[:, hd * Q_HEAD_COLS:hd * Q_HEAD_COLS + NOPE_DIM].astype(BF16)
            q_lat = _dot(q_nope, wukt_ref[hd]) * Q_SCALE
            qs_ref[hd, :, :KV_RANK] = q_lat.astype(BF16)
            qs_ref[hd, :, KV_RANK:] = q_rope(hd)


def _mla_project(x, modp, mods, g, w_in_ext, q_norm, w_uq_ext, kv_norm, w_uk_t, w_uk_flat,
                 w_uv_t, table_p, table_s):
    return pl.pallas_call(
        _mla_proj_kernel,
        grid=(N_TILES,),
        in_specs=[_row_spec()] + _mod_specs() + [
            _resident((1, D_MODEL)),
            _resident(w_in_ext.shape),
            _resident((1, Q_RANK)),
            _resident(w_uq_ext.shape),
            _resident((1, KV_RANK)),
            _resident(w_uk_t.shape),
            _resident(w_uk_flat.shape),
            _resident(w_uv_t.shape),
            pl.BlockSpec((TM, 2 * ROPE_DIM), lambda i: (i % TILES_PER_SEQ, 0)),
            _resident((TM, 2 * ROPE_DIM)),
        ],
        out_specs=[
            _prow_spec(KV_RANK),
            _prow_spec(ROPE_DIM),
            _srow_spec(KV_RANK),
            _srow_spec(ROPE_DIM),
            pl.BlockSpec((N_HEADS, TM, HEAD_DIM), lambda i: (0, _ptile(i), 0)),
            pl.BlockSpec((N_HEADS, TM, HEAD_DIM), lambda i: (0, _ptile(i), 0)),
            pl.BlockSpec((TM // TK, N_HEADS * V_DIM, TK), lambda i: (_ptile(i), 0, 0)),
            pl.BlockSpec((N_HEADS, TM, QK_DIM), lambda i: (0, _stile(i), 0)),
        ],
        out_shape=[
            jax.ShapeDtypeStruct((P_ROWS, KV_RANK), F32),
            jax.ShapeDtypeStruct((P_ROWS, ROPE_DIM), F32),
            jax.ShapeDtypeStruct((S_ROWS, KV_RANK), F32),
            jax.ShapeDtypeStruct((S_ROWS, ROPE_DIM), F32),
            jax.ShapeDtypeStruct((N_HEADS, P_ROWS, HEAD_DIM), BF16),
            jax.ShapeDtypeStruct((N_HEADS, P_ROWS, HEAD_DIM), BF16),
            jax.ShapeDtypeStruct((P_ROWS // TK, N_HEADS * V_DIM, TK), BF16),
            jax.ShapeDtypeStruct((N_HEADS, S_ROWS, QK_DIM), BF16),
        ],
        compiler_params=_params(("arbitrary",)),
        name="mla_project",
    )(x, modp, mods, g, w_in_ext, q_norm, w_uq_ext, kv_norm, w_uk_t, w_uk_flat, w_uv_t,
      table_p, table_s)


def _attn_prompt_kernel(q_ref, k_ref, vt_ref, o_ref, m_s, l_s, acc_s, a_s, p_s):
    qi = pl.program_id(1)
    m_s[...] = jnp.full_like(m_s, -jnp.inf)
    l_s[...] = jnp.zeros_like(l_s)
    acc_s[...] = jnp.zeros_like(acc_s)

    def step(kb0, n_kb, masked):
        keys = n_kb * TK
        off = pl.multiple_of(kb0 * TK, TK)
        for hd in range(N_HEADS):
            s = _dot_nt(k_ref[hd, pl.ds(off, keys), :], q_ref[hd])
            if masked:
                k_pos = off + lax.broadcasted_iota(jnp.int32, (keys, TQ), 0)
                q_pos = qi * TQ + lax.broadcasted_iota(jnp.int32, (keys, TQ), 1)
                s = jnp.where(k_pos <= q_pos, s, -jnp.inf)
            m_old = m_s[hd]
            m_new = jnp.maximum(m_old, jnp.max(s, axis=0, keepdims=True))
            alpha = jnp.exp2(m_old - m_new)
            p = jnp.exp2(s - m_new)
            l_s[hd] = alpha * l_s[hd] + jnp.sum(p, axis=0, keepdims=True)
            p_s[hd, :keys] = p.astype(BF16)
            a_s[hd] = alpha
            m_s[hd] = m_new
        for hd in range(N_HEADS):
            pv = _dot(vt_ref[kb0, hd * V_DIM:(hd + 1) * V_DIM, :], p_s[hd, :TK])
            for j in range(1, n_kb):
                pv += _dot(vt_ref[kb0 + j, hd * V_DIM:(hd + 1) * V_DIM, :],
                           p_s[hd, j * TK:(j + 1) * TK])
            acc_s[hd] = a_s[hd] * acc_s[hd] + pv

    def body(j, carry):
        step(2 * j, 2, False)
        return carry

    lax.fori_loop(0, qi // 2, body, 0)

    @pl.when(qi % 2 == 1)
    def _():
        step(qi - 1, 1, False)

    step(qi, 1, True)
    for hd in range(N_HEADS):
        o_t = acc_s[hd] / l_s[hd]
        o_ref[:, hd * V_DIM:(hd + 1) * V_DIM] = o_t.T.astype(BF16)


def _attn_prompt(q_p, k_p, v_t):
    nq = SEQ // TQ
    return pl.pallas_call(
        _attn_prompt_kernel,
        grid=(BATCH, nq),
        in_specs=[
            pl.BlockSpec((N_HEADS, TQ, HEAD_DIM), lambda b, i: (0, b * nq + i, 0)),
            pl.BlockSpec((N_HEADS, SEQ, HEAD_DIM), lambda b, i: (0, b, 0)),
            pl.BlockSpec((N_KBLOCKS, N_HEADS * V_DIM, TK), lambda b, i: (b, 0, 0)),
        ],
        out_specs=pl.BlockSpec((TQ, N_HEADS * V_DIM), lambda b, i: (b * nq + i, 0)),
        out_shape=jax.ShapeDtypeStruct((P_ROWS, N_HEADS * V_DIM), BF16),
        scratch_shapes=[pltpu.VMEM((N_HEADS, 1, TQ), F32), pltpu.VMEM((N_HEADS, 1, TQ), F32),
                        pltpu.VMEM((N_HEADS, V_DIM, TQ), F32), pltpu.VMEM((N_HEADS, 1, TQ), F32),
                        pltpu.VMEM((N_HEADS, 2 * TK, TQ), BF16)],
        compiler_params=_params(("arbitrary", "arbitrary")),
        name="attn_prompt",
    )(q_p, k_p, v_t)


def _attn_sample_kernel(layer, pt_ref, q_ref, ln_ref, rn_ref, lat_hbm, rope_hbm, o_ref,
                        lat_buf, rope_buf, kl_s, kr_s, sem):
    b = pl.program_id(0)
    slot = lax.rem(b, 2)
    rows = N_HEADS * DEC_SEQ

    def page_copies(page, sl, p):
        return (pltpu.make_async_copy(lat_hbm.at[layer, page], lat_buf.at[sl, p], sem.at[sl, 0]),
                pltpu.make_async_copy(rope_hbm.at[layer, page], rope_buf.at[sl, p], sem.at[sl, 1]))

    def start_pages(seq, sl):
        def body(p, carry):
            for cp in page_copies(pt_ref[seq, p], sl, p):
                cp.start()
            return carry
        lax.fori_loop(0, N_PAGES, body, 0, unroll=8)

    @pl.when(b == 0)
    def _():
        start_pages(0, 0)

    @pl.when(b + 1 < DEC_BATCH)
    def _():
        start_pages(b + 1, 1 - slot)

    for p in range(N_PAGES):
        for cp in page_copies(0, slot, p):
            cp.wait()

    for p in range(N_PAGES):
        kl_s[p * PAGE_SIZE:(p + 1) * PAGE_SIZE, :] = lat_buf[slot, p].astype(BF16)
        kr_s[:, p * PAGE_SIZE:(p + 1) * PAGE_SIZE] = rope_buf[slot, p].astype(BF16)

    q = q_ref[0]
    q_lat = q[:, :KV_RANK]
    q_rope = q[:, KV_RANK:]
    k_lat = kl_s[...]
    s = _dot_nt(q_lat, k_lat) + _dot(q_rope, kr_s[...])
    new_lat = ln_ref[...].astype(BF16)
    new_rope = rn_ref[...].astype(BF16)
    s_new = _dot_nt(q_lat, new_lat) + _dot_nt(q_rope, new_rope)
    q_pos = lax.broadcasted_iota(jnp.int32, (rows, DEC_SEQ), 0) & (DEC_SEQ - 1)
    k_pos = lax.broadcasted_iota(jnp.int32, (rows, DEC_SEQ), 1)
    s_new = jnp.where(k_pos <= q_pos, s_new, -jnp.inf)
    m = jnp.maximum(jnp.max(s, axis=-1, keepdims=True), jnp.max(s_new, axis=-1, keepdims=True))
    p = jnp.exp2(s - m)
    p_new = jnp.exp2(s_new - m)
    l = jnp.sum(p, axis=-1, keepdims=True) + jnp.sum(p_new, axis=-1, keepdims=True)
    pb = p.astype(BF16)
    half = KV_RANK // 2
    ctx = jnp.concatenate([_dot(pb, kl_s[:, :half]), _dot(pb, kl_s[:, half:])], axis=1)
    ctx = (ctx + _dot(p_new.astype(BF16), new_lat)) / l
    for hd in range(N_HEADS):
        o_ref[0, :, hd * KV_RANK:(hd + 1) * KV_RANK] = ctx[hd * DEC_SEQ:(hd + 1) * DEC_SEQ]


def _attn_sample(layer, page_table, q_s, lat, k_rope, cache_lat, cache_rope_t):
    rows = N_HEADS * DEC_SEQ
    return pl.pallas_call(
        functools.partial(_attn_sample_kernel, layer),
        grid_spec=pltpu.PrefetchScalarGridSpec(
            num_scalar_prefetch=1,
            grid=(DEC_BATCH,),
            in_specs=[
                pl.BlockSpec((1, rows, QK_DIM), lambda b, pt: (b, 0, 0)),
                pl.BlockSpec((DEC_SEQ, KV_RANK), lambda b, pt: (b, 0)),
                pl.BlockSpec((DEC_SEQ, ROPE_DIM), lambda b, pt: (b, 0)),
                pl.BlockSpec(memory_space=pl.ANY),
                pl.BlockSpec(memory_space=pl.ANY),
            ],
            out_specs=pl.BlockSpec((1, DEC_SEQ, N_HEADS * KV_RANK), lambda b, pt: (b, 0, 0)),
            scratch_shapes=[
                pltpu.VMEM((2, N_PAGES, PAGE_SIZE, KV_RANK), F32),
                pltpu.VMEM((2, N_PAGES, ROPE_DIM, PAGE_SIZE), F32),
                pltpu.VMEM((PAST_LEN, KV_RANK), BF16),
                pltpu.VMEM((ROPE_DIM, PAST_LEN), BF16),
                pltpu.SemaphoreType.DMA((2, 2)),
            ],
        ),
        out_shape=jax.ShapeDtypeStruct((DEC_BATCH, DEC_SEQ, N_HEADS * KV_RANK), F32),
        compiler_params=_params(("arbitrary",)),
        name="attn_sample",
    )(page_table, q_s, lat, k_rope, cache_lat, cache_rope_t)


def _mla_out_kernel(x_ref, modp_ref, mods_ref, op_ref, cs_ref, wuv_ref, wo_ref, o_ref, o_s):
    is_p = pl.program_id(0) < N_PTILES

    @pl.when(is_p)
    def _():
        o_s[...] = op_ref[...]

    @pl.when(jnp.logical_not(is_p))
    def _():
        for hd in range(N_HEADS):
            ctx = cs_ref[:, hd * KV_RANK:(hd + 1) * KV_RANK].astype(BF16)
            o_s[:, hd * V_DIM:(hd + 1) * V_DIM] = _dot(ctx, wuv_ref[hd]).astype(BF16)

    y = _dot(o_s[...], wo_ref[...])
    o_ref[...] = _gated(x_ref[...], _mod_vec(modp_ref, mods_ref, 2), y)


def _mla_output(x, modp, mods, o_p, ctx_s, w_uv_h, w_o):
    row = pl.BlockSpec((TM, D_MODEL), lambda i: (i, 0))
    return pl.pallas_call(
        _mla_out_kernel,
        grid=(N_TILES,),
        in_specs=[row] + _mod_specs() + [
            pl.BlockSpec((TM, N_HEADS * V_DIM), lambda i: (_ptile(i), 0)),
            pl.BlockSpec((TM, N_HEADS * KV_RANK), lambda i: (_stile(i), 0)),
            _resident(w_uv_h.shape),
            _resident(w_o.shape),
        ],
        out_specs=row,
        out_shape=jax.ShapeDtypeStruct((ROWS, D_MODEL), F32),
        scratch_shapes=[pltpu.VMEM((TM, N_HEADS * V_DIM), BF16)],
        compiler_params=_params(("arbitrary",)),
        name="mla_output",
    )(x, modp, mods, o_p, ctx_s, w_uv_h, w_o)


def _gmlp_kernel(x_ref, modp_ref, mods_ref, g_ref, win_ref, bin_ref, lng_ref, lnb_ref,
                 wmp_ref, wms_ref, bp_ref, bs_ref, wout_ref, o_ref, vp_ref, vs_ref, vb_s, t_s):
    i = pl.program_id(0)
    is_p = i < N_PTILES
    x = x_ref[...]
    h = _modulated(x, g_ref, modp_ref, mods_ref)
    z = _dot(h, win_ref[...]) + bin_ref[...]
    z = z / (1 + jnp.exp2(z * (GELU_A + GELU_B * (z * z))))
    v = z[:, GMLP_DIM:]
    mu = jnp.mean(v, axis=-1, keepdims=True)
    var = jnp.mean(jnp.square(v - mu), axis=-1, keepdims=True)
    vn = (v - mu) * lax.rsqrt(var + EPS) * lng_ref[...] + lnb_ref[...]

    @pl.when(jnp.logical_and(is_p, i % TILES_PER_SEQ == TILES_PER_SEQ - 1))
    def _():
        vp_ref[...] = vn[TM - CHUNK:]

    @pl.when(jnp.logical_not(is_p))
    def _():
        vs_ref[...] = vn

    vb_s[...] = vn.astype(BF16)
    r = lax.broadcasted_iota(jnp.int32, (CHUNK, CHUNK), 0)
    c = lax.broadcasted_iota(jnp.int32, (CHUNK, CHUNK), 1)
    same_seq = jnp.logical_or(is_p, jnp.bitwise_xor(r, c) < DEC_SEQ)
    keep = jnp.logical_and(r >= c, same_seq)
    bias = jnp.where(is_p, bp_ref[...], bs_ref[...])
    for g in range(GMLP_GROUPS):
        cols = slice(g * GROUP_DIM, (g + 1) * GROUP_DIM)
        w = jnp.where(keep, jnp.where(is_p, wmp_ref[g], wms_ref[g]), 0.0).astype(BF16)
        for ch in range(TM // CHUNK):
            rws = slice(ch * CHUNK, (ch + 1) * CHUNK)
            mixed = _dot(w, vb_s[rws, cols]) + bias[:, cols]
            t_s[rws, cols] = (z[rws, cols] * mixed).astype(BF16)
    y = _dot(t_s[...], wout_ref[...])
    o_ref[...] = _gated(x, _mod_vec(modp_ref, mods_ref, 2), y)


def _gmlp_sublayer(x, modp, mods, g, w_in, b_in, ln_g, ln_b, wm_p, wm_s, bias_p, bias_s, w_out):
    row = pl.BlockSpec((TM, D_MODEL), lambda i: (i, 0))
    return pl.pallas_call(
        _gmlp_kernel,
        grid=(N_TILES,),
        in_specs=[row] + _mod_specs() + [
            _resident((1, D_MODEL)),
            _resident(w_in.shape),
            _resident((1, 2 * GMLP_DIM)),
            _resident((1, GMLP_DIM)),
            _resident((1, GMLP_DIM)),
            _resident(wm_p.shape),
            _resident(wm_s.shape),
            _resident(bias_p.shape),
            _resident(bias_s.shape),
            _resident(w_out.shape),
        ],
        out_specs=[
            row,
            pl.BlockSpec((CHUNK, GMLP_DIM),
                         lambda i: (jnp.minimum(i // TILES_PER_SEQ, BATCH - 1), 0)),
            _srow_spec(GMLP_DIM),
        ],
        out_shape=[jax.ShapeDtypeStruct((ROWS, D_MODEL), F32),
                   jax.ShapeDtypeStruct((BATCH * CHUNK, GMLP_DIM), F32),
                   jax.ShapeDtypeStruct((S_ROWS, GMLP_DIM), F32)],
        scratch_shapes=[pltpu.VMEM((TM, GMLP_DIM), BF16), pltpu.VMEM((TM, GMLP_DIM), BF16)],
        compiler_params=_params(("arbitrary",)),
        name="gmlp_sublayer",
    )(x, modp, mods, g, w_in, b_in, ln_g, ln_b, wm_p, wm_s, bias_p, bias_s, w_out)


def _rope_table(pos):
    half = ROPE_DIM // 2
    freqs = jnp.power(ROPE_THETA, -jnp.arange(half, dtype=F32) / half)
    ang = pos[:, None] * freqs[None, :]
    cos, sin = jnp.cos(ang), jnp.sin(ang)
    return jnp.concatenate([cos, cos, -sin, sin], axis=-1)


def _with_rotated(w):
    half = ROPE_DIM // 2
    return jnp.concatenate([w, w[..., -half:], w[..., -ROPE_DIM:-half]], axis=-1)


def _split_mods(mod_l):
    out = []
    for k in range(N_SUB):
        m = mod_l[3 * k:3 * k + 3]
        out.append((m[:, :BATCH], m[:, BATCH:, None, :]))
    return out


def kernel(x_prompt, x_sample, cache_kv_latent, cache_k_rope, page_table, c_prompt, c_sample,
           ada_w, ada_b, norm_g, ffn_w_in, ffn_w_out,
           a_w_in, a_q_norm, a_w_uq, a_kv_norm, a_w_uk, a_w_uv, a_w_o,
           b_w_in, b_b_in, b_ln_g, b_ln_b, b_w_s, b_b_s, b_w_out, final_g):
    xs = (x_prompt.reshape(P_ROWS, D_MODEL), x_sample.reshape(S_ROWS, D_MODEL))
    c_all = jnp.concatenate([c_prompt, c_sample])
    mod = _ada_modulation(c_all, ada_w, ada_b)
    table_p = _rope_table(jnp.arange(SEQ, dtype=F32))
    table_s = _rope_table(jnp.tile(jnp.arange(DEC_SEQ, dtype=F32) + PAST_LEN, TM // DEC_SEQ))
    cache_rope_t = cache_k_rope.transpose(0, 1, 3, 2)

    lat_p, rope_p, lat_s, rope_s, v_p, v_s = [], [], [], [], [], []
    for i in range(DEPTH):
        mods_i = _split_mods(mod[i])
        j = i // 2
        x = _ffn_sublayer(xs if i == 0 else (x,), *mods_i[0], norm_g[i, 0][None], ffn_w_in, ffn_w_out, i, 0)
        if i % 2 == 0:
            w_in_ext = _with_rotated(a_w_in[j]).astype(BF16)
            w_uq_ext = _with_rotated(a_w_uq[j]).reshape(Q_RANK, N_HEADS * Q_HEAD_COLS).astype(BF16)
            w_uk_t = a_w_uk[j].transpose(1, 2, 0).astype(BF16)
            w_uk_flat = a_w_uk[j].reshape(KV_RANK, N_HEADS * NOPE_DIM).astype(BF16)
            w_uv_h = a_w_uv[j].transpose(1, 0, 2).astype(BF16)
            w_uv_t = a_w_uv[j].reshape(KV_RANK, N_HEADS * V_DIM).T.astype(BF16)
            l_p, r_p, l_s, r_s, q_p, k_p, v_t, q_s = _mla_project(
                x, *mods_i[1], norm_g[i, 1][None], w_in_ext, a_q_norm[j][None], w_uq_ext,
                a_kv_norm[j][None], w_uk_t, w_uk_flat, w_uv_t, table_p, table_s)
            o_p = _attn_prompt(q_p, k_p, v_t)
            q_s = q_s.reshape(N_HEADS, DEC_BATCH, DEC_SEQ, QK_DIM)
            q_s = q_s.transpose(1, 0, 2, 3).reshape(DEC_BATCH, N_HEADS * DEC_SEQ, QK_DIM)
            ctx_s = _attn_sample(j, page_table, q_s, l_s, r_s, cache_kv_latent, cache_rope_t)
            x = _mla_output(x, *mods_i[1], o_p, ctx_s.reshape(S_ROWS, N_HEADS * KV_RANK),
                            w_uv_h, a_w_o[j].astype(BF16))
            lat_p.append(l_p.reshape(BATCH, SEQ, KV_RANK))
            rope_p.append(r_p.reshape(BATCH, SEQ, ROPE_DIM))
            lat_s.append(l_s.reshape(DEC_BATCH, DEC_SEQ, KV_RANK))
            rope_s.append(r_s.reshape(DEC_BATCH, DEC_SEQ, ROPE_DIM))
        else:
            wm_s = jnp.tile(b_w_s[j][:, :DEC_SEQ, :DEC_SEQ], (1, CHUNK // DEC_SEQ, CHUNK // DEC_SEQ))
            bias_p = jnp.repeat(b_b_s[j].T, GROUP_DIM, axis=1)
            bias_s = jnp.tile(bias_p[:DEC_SEQ], (CHUNK // DEC_SEQ, 1))
            x, vp, vs = _gmlp_sublayer(
                x, *mods_i[1], norm_g[i, 1][None], b_w_in[j].astype(BF16), b_b_in[j][None],
                b_ln_g[j][None], b_ln_b[j][None], b_w_s[j], wm_s, bias_p, bias_s,
                b_w_out[j].astype(BF16))
            v_p.append(vp.reshape(BATCH, CHUNK, GMLP_DIM))
            v_s.append(vs.reshape(DEC_BATCH, DEC_SEQ, GMLP_DIM))
        x = _ffn_sublayer((x,), *mods_i[2], norm_g[i, 2][None], ffn_w_in, ffn_w_out, i, 1,
                          final_g=final_g[None] if i == DEPTH - 1 else None)

    y_p, y_s = x
    return (
        y_p.reshape(BATCH, SEQ, D_MODEL),
        y_s.reshape(DEC_BATCH, DEC_SEQ, D_MODEL),
        jnp.stack(lat_p), jnp.stack(rope_p), jnp.stack(lat_s), jnp.stack(rope_s),
        jnp.stack(v_p), jnp.stack(v_s),
    )
```

```python
import functools

import jax
import jax.numpy as jnp
from jax import lax
from jax.experimental import pallas as pl
from jax.experimental.pallas import tpu as pltpu

D_MODEL = 1024
BATCH = 8
SEQ = 2048
DEPTH = 4
DEC_BATCH = 128
DEC_SEQ = 8
PAST_LEN = 8192
PAGE_SIZE = 128
N_PAGES = PAST_LEN // PAGE_SIZE
N_SUB = 3
HALF_STEP = 0.5
EPS = 1e-6
N_HEADS = 8
NOPE_DIM = 128
ROPE_DIM = 64
V_DIM = 128
Q_RANK = 384
KV_RANK = 256
ROPE_THETA = 10000.0
ATTN_SCALE = (NOPE_DIM + ROPE_DIM) ** -0.5
LOG2_E = 1.4426950408889634
GELU_A = -2.0 * 0.7978845608028654 * LOG2_E
GELU_B = GELU_A * 0.044715
Q_SCALE = ATTN_SCALE * LOG2_E
CHUNK = 128
GMLP_DIM = D_MODEL
GMLP_GROUPS = 8
GROUP_DIM = GMLP_DIM // GMLP_GROUPS
FFN_DIM = 2816

P_ROWS = BATCH * SEQ
S_ROWS = DEC_BATCH * DEC_SEQ
ROWS = P_ROWS + S_ROWS
TM = 512
N_TILES = ROWS // TM
TILE_SEQS = TM // DEC_SEQ
N_PTILES = P_ROWS // TM
TILES_PER_SEQ = SEQ // TM
FFN_CHUNK = 256
N_FFN_CHUNKS = FFN_DIM // FFN_CHUNK
N_W_CHUNKS = 11
W_IN_COLS = 2 * FFN_DIM // N_W_CHUNKS
W_OUT_ROWS = FFN_DIM // N_W_CHUNKS
QK_DIM = KV_RANK + ROPE_DIM
HEAD_DIM = NOPE_DIM + ROPE_DIM
Q_HEAD_COLS = 2 * NOPE_DIM
TQ = 256
TK = 256
N_KBLOCKS = SEQ // TK
PAGE_SLOTS = 3
VMEM_LIMIT = 56 * 1024 * 1024

F32 = jnp.float32
BF16 = jnp.bfloat16


def _params(sem):
    return pltpu.CompilerParams(dimension_semantics=sem, vmem_limit_bytes=VMEM_LIMIT)


def _resident(shape):
    nd = len(shape)
    return pl.BlockSpec(shape, lambda *_: (0,) * nd, pipeline_mode=pl.Buffered(1))


def _ptile(i):
    return jnp.minimum(i, N_PTILES - 1)


def _stile(i):
    return jnp.maximum(i - N_PTILES, 0)


def _mod_specs():
    modp = _resident((3, BATCH, D_MODEL))
    mods = pl.BlockSpec((3, TILE_SEQS, 1, D_MODEL), lambda i: (0, _stile(i), 0, 0))
    return [modp, mods]


def _mod_vec(modp_ref, mods_ref, j):
    i = pl.program_id(0)
    b = jnp.minimum(i // TILES_PER_SEQ, BATCH - 1)
    return jnp.where(i < N_PTILES, modp_ref[j, pl.ds(b, 1), :][None], mods_ref[j])


def _per_seq(a):
    return a.reshape(TILE_SEQS, DEC_SEQ, a.shape[-1])


def _rms(x):
    return x * lax.rsqrt(jnp.mean(x * x, axis=-1, keepdims=True) + EPS)


def _modulated(x, g_ref, modp_ref, mods_ref):
    shift = _mod_vec(modp_ref, mods_ref, 0)
    scale = _mod_vec(modp_ref, mods_ref, 1)
    y = _per_seq(_rms(x) * g_ref[...]) * (1 + scale) + shift
    return y.reshape(x.shape).astype(BF16)


def _gated(x, gate, y):
    return (_per_seq(x) + gate * _per_seq(y)).reshape(x.shape)


def _dot(a, b):
    return jnp.dot(a, b, preferred_element_type=F32)


def _dot_nt(a, b):
    return lax.dot_general(a, b, (((1,), (1,)), ((), ())), preferred_element_type=F32)


def _mod_kernel(c_ref, w_ref, b_ref, o_ref):
    c = c_ref[...]
    a = (c * jax.nn.sigmoid(c)).astype(BF16)
    o_ref[0, 0] = _dot(a, w_ref[0].astype(BF16)) + b_ref[0]


def _ada_modulation(c_all, ada_w, ada_b):
    n = c_all.shape[0]
    width = N_SUB * 3 * D_MODEL
    return pl.pallas_call(
        _mod_kernel,
        grid=(DEPTH, N_SUB * 3),
        in_specs=[
            pl.BlockSpec((n, D_MODEL), lambda l, j: (0, 0)),
            pl.BlockSpec((1, D_MODEL, D_MODEL), lambda l, j: (l, 0, j)),
            pl.BlockSpec((1, 1, D_MODEL), lambda l, j: (l, 0, j)),
        ],
        out_specs=pl.BlockSpec((1, 1, n, D_MODEL), lambda l, j: (l, j, 0, 0)),
        out_shape=jax.ShapeDtypeStruct((DEPTH, N_SUB * 3, n, D_MODEL), F32),
        compiler_params=_params(("arbitrary", "arbitrary")),
        name="ada_modulation",
    )(c_all, ada_w, ada_b.reshape(DEPTH, 1, width))


def _ffn_kernel(n_x, final, layer, k, *refs):
    x_refs, refs = refs[:n_x], refs[n_x:]
    modp_ref, mods_ref, g_ref, win_hbm, wo_hbm = refs[:5]
    refs = refs[5:]
    h_s, a_s, win_s, wo_s, stin_s, sto_s, sem = refs[-7:]
    is_p = pl.program_id(0) < N_PTILES

    @pl.when(pl.program_id(0) == 0)
    def _():
        def in_copy(c):
            return pltpu.make_async_copy(
                win_hbm.at[layer, k, :, pl.ds(c * W_IN_COLS, W_IN_COLS)],
                stin_s.at[c % 2], sem.at[0, c % 2])

        def out_copy(c):
            return pltpu.make_async_copy(
                wo_hbm.at[layer, k, pl.ds(c * W_OUT_ROWS, W_OUT_ROWS), :],
                sto_s.at[c % 2], sem.at[1, c % 2])

        in_copy(0).start()
        out_copy(0).start()
        for c in range(N_W_CHUNKS):
            if c + 1 < N_W_CHUNKS:
                in_copy(c + 1).start()
                out_copy(c + 1).start()
            in_copy(c).wait()
            win_s[:, c * W_IN_COLS:(c + 1) * W_IN_COLS] = stin_s[c % 2].astype(BF16)
            out_copy(c).wait()
            wo_s[c * W_OUT_ROWS:(c + 1) * W_OUT_ROWS, :] = sto_s[c % 2].astype(BF16)

    x = x_refs[0][...] if n_x == 1 else jnp.where(is_p, x_refs[0][...], x_refs[1][...])
    h_s[...] = _modulated(x, g_ref, modp_ref, mods_ref)
    for c in range(N_FFN_CHUNKS):
        h = h_s[...]
        gate = _dot(h, win_s[:, c * FFN_CHUNK:(c + 1) * FFN_CHUNK])
        up = _dot(h, win_s[:, FFN_DIM + c * FFN_CHUNK:FFN_DIM + (c + 1) * FFN_CHUNK])
        a_s[:, c * FFN_CHUNK:(c + 1) * FFN_CHUNK] = (gate * jax.nn.sigmoid(gate) * up).astype(BF16)
    y = _dot(a_s[...], wo_s[...])
    out = _gated(x, HALF_STEP * _mod_vec(modp_ref, mods_ref, 2), y)
    if not final:
        refs[0][...] = out
        return
    fg_ref, yp_ref, ys_ref = refs[:3]
    normed = _rms(out) * fg_ref[...]

    @pl.when(is_p)
    def _():
        yp_ref[...] = normed

    @pl.when(jnp.logical_not(is_p))
    def _():
        ys_ref[...] = normed


def _row_spec(width=D_MODEL):
    return pl.BlockSpec((TM, width), lambda i: (i, 0))


def _prow_spec(width=D_MODEL):
    return pl.BlockSpec((TM, width), lambda i: (_ptile(i), 0))


def _srow_spec(width=D_MODEL):
    return pl.BlockSpec((TM, width), lambda i: (_stile(i), 0))


def _ffn_sublayer(xs, modp, mods, g, w_in, w_out, layer, k, final_g=None):
    final = final_g is not None
    x_specs = [_row_spec()] if len(xs) == 1 else [_prow_spec(), _srow_spec()]
    in_specs = x_specs + _mod_specs() + [
        _resident((1, D_MODEL)),
        pl.BlockSpec(memory_space=pl.ANY),
        pl.BlockSpec(memory_space=pl.ANY),
    ]
    args = list(xs) + [modp, mods, g, w_in, w_out]
    if final:
        in_specs.append(_resident((1, D_MODEL)))
        args.append(final_g)
        out_specs = [_prow_spec(), _srow_spec()]
        out_shape = [jax.ShapeDtypeStruct((P_ROWS, D_MODEL), F32),
                     jax.ShapeDtypeStruct((S_ROWS, D_MODEL), F32)]
    else:
        out_specs = _row_spec()
        out_shape = jax.ShapeDtypeStruct((ROWS, D_MODEL), F32)
    return pl.pallas_call(
        functools.partial(_ffn_kernel, len(xs), final, layer, k),
        grid=(N_TILES,),
        in_specs=in_specs,
        out_specs=out_specs,
        out_shape=out_shape,
        scratch_shapes=[
            pltpu.VMEM((TM, D_MODEL), BF16), pltpu.VMEM((TM, FFN_DIM), BF16),
            pltpu.VMEM((D_MODEL, 2 * FFN_DIM), BF16), pltpu.VMEM((FFN_DIM, D_MODEL), BF16),
            pltpu.VMEM((2, D_MODEL, W_IN_COLS), F32), pltpu.VMEM((2, W_OUT_ROWS, D_MODEL), F32),
            pltpu.SemaphoreType.DMA((2, 2)),
        ],
        compiler_params=_params(("arbitrary",)),
        name="ffn_sublayer",
    )(*args)


def _rope(t, table):
    w = t * table
    return w + pltpu.roll(w, ROPE_DIM, axis=1)


def _mla_proj_kernel(x_ref, modp_ref, mods_ref, g_ref, win_ref, qn_ref, wuq_ref, kvn_ref,
                     wukt_ref, wuk_ref, wuvt_ref, tabp_ref, tabs_ref,
                     latp_ref, krp_ref, lats_ref, krs_ref, qp_ref, kp_ref, vt_ref, qs_ref):
    is_p = pl.program_id(0) < N_PTILES
    h = _modulated(x_ref[...], g_ref, modp_ref, mods_ref)
    z = _dot(h, win_ref[...])
    table = jnp.where(is_p, tabp_ref[...], tabs_ref[...])
    k_rope = _rope(z[:, Q_RANK + KV_RANK:], table)[:, :ROPE_DIM]
    latent = _rms(z[:, Q_RANK:Q_RANK + KV_RANK]) * kvn_ref[...]
    qn = (_rms(z[:, :Q_RANK]) * qn_ref[...]).astype(BF16)
    q_all = _dot(qn, wuq_ref[...])

    def q_rope(hd):
        t = q_all[:, hd * Q_HEAD_COLS + NOPE_DIM:(hd + 1) * Q_HEAD_COLS]
        return (_rope(t, table)[:, :ROPE_DIM] * Q_SCALE).astype(BF16)

    @pl.when(is_p)
    def _():
        latp_ref[...] = latent
        krp_ref[...] = k_rope
        lat_b = latent.astype(BF16)
        k_rope_b = k_rope.astype(BF16)
        k_nope = _dot(lat_b, wuk_ref[...])
        v_t = _dot_nt(wuvt_ref[...], lat_b).astype(BF16)
        for kb in range(TM // TK):
            vt_ref[kb] = v_t[:, kb * TK:(kb + 1) * TK]
        for hd in range(N_HEADS):
            q_nope = q_all[:, hd * Q_HEAD_COLS:hd * Q_HEAD_COLS + NOPE_DIM]
            qp_ref[hd, :, :NOPE_DIM] = (q_nope * Q_SCALE).astype(BF16)
            qp_ref[hd, :, NOPE_DIM:] = q_rope(hd)
            kp_ref[hd, :, :NOPE_DIM] = k_nope[:, hd * NOPE_DIM:(hd + 1) * NOPE_DIM].astype(BF16)
            kp_ref[hd, :, NOPE_DIM:] = k_rope_b

    @pl.when(jnp.logical_not(is_p))
    def _():
        lats_ref[...] = latent
        krs_ref[...] = k_rope
        for hd in range(N_HEADS):
            q_nope = q_all[:, hd * Q_HEAD_COLS:hd * Q_HEAD_COLS + NOPE_DIM].astype(BF16)
            q_lat = _dot(q_nope, wukt_ref[hd]) * Q_SCALE
            qs_ref[hd, :, :KV_RANK] = q_lat.astype(BF16)
            qs_ref[hd, :, KV_RANK:] = q_rope(hd)


def _mla_project(x, modp, mods, g, w_in_ext, q_norm, w_uq_ext, kv_norm, w_uk_t, w_uk_flat,
                 w_uv_t, table_p, table_s):
    return pl.pallas_call(
        _mla_proj_kernel,
        grid=(N_TILES,),
        in_specs=[_row_spec()] + _mod_specs() + [
            _resident((1, D_MODEL)),
            _resident(w_in_ext.shape),
            _resident((1, Q_RANK)),
            _resident(w_uq_ext.shape),
            _resident((1, KV_RANK)),
            _resident(w_uk_t.shape),
            _resident(w_uk_flat.shape),
            _resident(w_uv_t.shape),
            pl.BlockSpec((TM, 2 * ROPE_DIM), lambda i: (i % TILES_PER_SEQ, 0)),
            _resident((TM, 2 * ROPE_DIM)),
        ],
        out_specs=[
            _prow_spec(KV_RANK),
            _prow_spec(ROPE_DIM),
            _srow_spec(KV_RANK),
            _srow_spec(ROPE_DIM),
            pl.BlockSpec((N_HEADS, TM, HEAD_DIM), lambda i: (0, _ptile(i), 0)),
            pl.BlockSpec((N_HEADS, TM, HEAD_DIM), lambda i: (0, _ptile(i), 0)),
            pl.BlockSpec((TM // TK, N_HEADS * V_DIM, TK), lambda i: (_ptile(i), 0, 0)),
            pl.BlockSpec((N_HEADS, TM, QK_DIM), lambda i: (0, _stile(i), 0)),
        ],
        out_shape=[
            jax.ShapeDtypeStruct((P_ROWS, KV_RANK), F32),
            jax.ShapeDtypeStruct((P_ROWS, ROPE_DIM), F32),
            jax.ShapeDtypeStruct((S_ROWS, KV_RANK), F32),
            jax.ShapeDtypeStruct((S_ROWS, ROPE_DIM), F32),
            jax.ShapeDtypeStruct((N_HEADS, P_ROWS, HEAD_DIM), BF16),
            jax.ShapeDtypeStruct((N_HEADS, P_ROWS, HEAD_DIM), BF16),
            jax.ShapeDtypeStruct((P_ROWS // TK, N_HEADS * V_DIM, TK), BF16),
            jax.ShapeDtypeStruct((N_HEADS, S_ROWS, QK_DIM), BF16),
        ],
        compiler_params=_params(("arbitrary",)),
        name="mla_project",
    )(x, modp, mods, g, w_in_ext, q_norm, w_uq_ext, kv_norm, w_uk_t, w_uk_flat, w_uv_t,
      table_p, table_s)


def _attn_prompt_kernel(q_ref, k_ref, vt_ref, o_ref, m_s, l_s, acc_s, a_s, p_s):
    qi = pl.program_id(1)
    m_s[...] = jnp.full_like(m_s, -jnp.inf)
    l_s[...] = jnp.zeros_like(l_s)
    acc_s[...] = jnp.zeros_like(acc_s)

    def step(kb0, n_kb, masked):
        keys = n_kb * TK
        off = pl.multiple_of(kb0 * TK, TK)
        for hd in range(N_HEADS):
            s = _dot_nt(k_ref[hd, pl.ds(off, keys), :], q_ref[hd])
            if masked:
                k_pos = off + lax.broadcasted_iota(jnp.int32, (keys, TQ), 0)
                q_pos = qi * TQ + lax.broadcasted_iota(jnp.int32, (keys, TQ), 1)
                s = jnp.where(k_pos <= q_pos, s, -jnp.inf)
            m_old = m_s[hd]
            m_new = jnp.maximum(m_old, jnp.max(s, axis=0, keepdims=True))
            alpha = jnp.exp2(m_old - m_new)
            p = jnp.exp2(s - m_new)
            l_s[hd] = alpha * l_s[hd] + jnp.sum(p, axis=0, keepdims=True)
            p_s[hd, :keys] = p.astype(BF16)
            a_s[hd] = alpha
            m_s[hd] = m_new
        for hd in range(N_HEADS):
            pv = _dot(vt_ref[kb0, hd * V_DIM:(hd + 1) * V_DIM, :], p_s[hd, :TK])
            for j in range(1, n_kb):
                pv += _dot(vt_ref[kb0 + j, hd * V_DIM:(hd + 1) * V_DIM, :],
                           p_s[hd, j * TK:(j + 1) * TK])
            acc_s[hd] = a_s[hd] * acc_s[hd] + pv

    def body(j, carry):
        step(2 * j, 2, False)
        return carry

    lax.fori_loop(0, qi // 2, body, 0)

    @pl.when(qi % 2 == 1)
    def _():
        step(qi - 1, 2, True)

    @pl.when(qi % 2 == 0)
    def _():
        step(qi, 1, True)
    for hd in range(N_HEADS):
        o_t = acc_s[hd] / l_s[hd]
        o_ref[:, hd * V_DIM:(hd + 1) * V_DIM] = o_t.T.astype(BF16)


def _attn_prompt(q_p, k_p, v_t):
    nq = SEQ // TQ
    return pl.pallas_call(
        _attn_prompt_kernel,
        grid=(BATCH, nq),
        in_specs=[
            pl.BlockSpec((N_HEADS, TQ, HEAD_DIM), lambda b, i: (0, b * nq + i, 0)),
            pl.BlockSpec((N_HEADS, SEQ, HEAD_DIM), lambda b, i: (0, b, 0)),
            pl.BlockSpec((N_KBLOCKS, N_HEADS * V_DIM, TK), lambda b, i: (b, 0, 0)),
        ],
        out_specs=pl.BlockSpec((TQ, N_HEADS * V_DIM), lambda b, i: (b * nq + i, 0)),
        out_shape=jax.ShapeDtypeStruct((P_ROWS, N_HEADS * V_DIM), BF16),
        scratch_shapes=[pltpu.VMEM((N_HEADS, 1, TQ), F32), pltpu.VMEM((N_HEADS, 1, TQ), F32),
                        pltpu.VMEM((N_HEADS, V_DIM, TQ), F32), pltpu.VMEM((N_HEADS, 1, TQ), F32),
                        pltpu.VMEM((N_HEADS, 2 * TK, TQ), BF16)],
        compiler_params=_params(("arbitrary", "arbitrary")),
        name="attn_prompt",
    )(q_p, k_p, v_t)


def _attn_sample_kernel(layer, pt_ref, q_ref, ln_ref, rn_ref, lat_hbm, rope_hbm, o_ref,
                        lat_buf, rope_buf, kl_s, kr_s, sem):
    b = pl.program_id(0)
    slot = lax.rem(b, PAGE_SLOTS)
    rows = N_HEADS * DEC_SEQ

    def page_copies(page, sl, p):
        return (pltpu.make_async_copy(lat_hbm.at[layer, page], lat_buf.at[sl, p], sem.at[sl, 0]),
                pltpu.make_async_copy(rope_hbm.at[layer, page], rope_buf.at[sl, p], sem.at[sl, 1]))

    def start_pages(seq, sl):
        def body(p, carry):
            for cp in page_copies(pt_ref[seq, p], sl, p):
                cp.start()
            return carry
        lax.fori_loop(0, N_PAGES, body, 0, unroll=True)

    @pl.when(b == 0)
    def _():
        for ahead in range(PAGE_SLOTS - 1):
            start_pages(ahead, ahead)

    @pl.when(b + PAGE_SLOTS - 1 < DEC_BATCH)
    def _():
        start_pages(b + PAGE_SLOTS - 1, lax.rem(b + PAGE_SLOTS - 1, PAGE_SLOTS))

    for p in range(N_PAGES):
        for cp in page_copies(0, slot, p):
            cp.wait()

    for p in range(N_PAGES):
        kl_s[p * PAGE_SIZE:(p + 1) * PAGE_SIZE, :] = lat_buf[slot, p].astype(BF16)
        kr_s[:, p * PAGE_SIZE:(p + 1) * PAGE_SIZE] = rope_buf[slot, p].astype(BF16)

    q = q_ref[0]
    q_lat = q[:, :KV_RANK]
    q_rope = q[:, KV_RANK:]
    k_lat = kl_s[...]
    s = _dot_nt(q_lat, k_lat) + _dot(q_rope, kr_s[...])
    new_lat = ln_ref[...].astype(BF16)
    new_rope = rn_ref[...].astype(BF16)
    s_new = _dot_nt(q_lat, new_lat) + _dot_nt(q_rope, new_rope)
    q_pos = lax.broadcasted_iota(jnp.int32, (rows, DEC_SEQ), 0) & (DEC_SEQ - 1)
    k_pos = lax.broadcasted_iota(jnp.int32, (rows, DEC_SEQ), 1)
    s_new = jnp.where(k_pos <= q_pos, s_new, -jnp.inf)
    m = jnp.maximum(jnp.max(s, axis=-1, keepdims=True), jnp.max(s_new, axis=-1, keepdims=True))
    p = jnp.exp2(s - m)
    p_new = jnp.exp2(s_new - m)
    l = jnp.sum(p, axis=-1, keepdims=True) + jnp.sum(p_new, axis=-1, keepdims=True)
    pb = p.astype(BF16)
    half = KV_RANK // 2
    ctx = jnp.concatenate([_dot(pb, kl_s[:, :half]), _dot(pb, kl_s[:, half:])], axis=1)
    ctx = (ctx + _dot(p_new.astype(BF16), new_lat)) / l
    for hd in range(N_HEADS):
        o_ref[0, :, hd * KV_RANK:(hd + 1) * KV_RANK] = ctx[hd * DEC_SEQ:(hd + 1) * DEC_SEQ]


def _attn_sample(layer, page_table, q_s, lat, k_rope, cache_lat, cache_rope_t):
    rows = N_HEADS * DEC_SEQ
    return pl.pallas_call(
        functools.partial(_attn_sample_kernel, layer),
        grid_spec=pltpu.PrefetchScalarGridSpec(
            num_scalar_prefetch=1,
            grid=(DEC_BATCH,),
            in_specs=[
                pl.BlockSpec((1, rows, QK_DIM), lambda b, pt: (b, 0, 0)),
                pl.BlockSpec((DEC_SEQ, KV_RANK), lambda b, pt: (b, 0)),
                pl.BlockSpec((DEC_SEQ, ROPE_DIM), lambda b, pt: (b, 0)),
                pl.BlockSpec(memory_space=pl.ANY),
                pl.BlockSpec(memory_space=pl.ANY),
            ],
            out_specs=pl.BlockSpec((1, DEC_SEQ, N_HEADS * KV_RANK), lambda b, pt: (b, 0, 0)),
            scratch_shapes=[
                pltpu.VMEM((PAGE_SLOTS, N_PAGES, PAGE_SIZE, KV_RANK), F32),
                pltpu.VMEM((PAGE_SLOTS, N_PAGES, ROPE_DIM, PAGE_SIZE), F32),
                pltpu.VMEM((PAST_LEN, KV_RANK), BF16),
                pltpu.VMEM((ROPE_DIM, PAST_LEN), BF16),
                pltpu.SemaphoreType.DMA((PAGE_SLOTS, 2)),
            ],
        ),
        out_shape=jax.ShapeDtypeStruct((DEC_BATCH, DEC_SEQ, N_HEADS * KV_RANK), F32),
        compiler_params=_params(("arbitrary",)),
        name="attn_sample",
    )(page_table, q_s, lat, k_rope, cache_lat, cache_rope_t)


def _mla_out_kernel(x_ref, modp_ref, mods_ref, op_ref, cs_ref, wuv_ref, wo_ref, o_ref, o_s):
    is_p = pl.program_id(0) < N_PTILES

    @pl.when(is_p)
    def _():
        o_s[...] = op_ref[...]

    @pl.when(jnp.logical_not(is_p))
    def _():
        for hd in range(N_HEADS):
            ctx = cs_ref[:, hd * KV_RANK:(hd + 1) * KV_RANK].astype(BF16)
            o_s[:, hd * V_DIM:(hd + 1) * V_DIM] = _dot(ctx, wuv_ref[hd]).astype(BF16)

    y = _dot(o_s[...], wo_ref[...])
    o_ref[...] = _gated(x_ref[...], _mod_vec(modp_ref, mods_ref, 2), y)


def _mla_output(x, modp, mods, o_p, ctx_s, w_uv_h, w_o):
    row = pl.BlockSpec((TM, D_MODEL), lambda i: (i, 0))
    return pl.pallas_call(
        _mla_out_kernel,
        grid=(N_TILES,),
        in_specs=[row] + _mod_specs() + [
            pl.BlockSpec((TM, N_HEADS * V_DIM), lambda i: (_ptile(i), 0)),
            pl.BlockSpec((TM, N_HEADS * KV_RANK), lambda i: (_stile(i), 0)),
            _resident(w_uv_h.shape),
            _resident(w_o.shape),
        ],
        out_specs=row,
        out_shape=jax.ShapeDtypeStruct((ROWS, D_MODEL), F32),
        scratch_shapes=[pltpu.VMEM((TM, N_HEADS * V_DIM), BF16)],
        compiler_params=_params(("arbitrary",)),
        name="mla_output",
    )(x, modp, mods, o_p, ctx_s, w_uv_h, w_o)


def _gmlp_kernel(x_ref, modp_ref, mods_ref, g_ref, win_ref, bin_ref, lng_ref, lnb_ref,
                 wmp_ref, wms_ref, bp_ref, bs_ref, wout_ref, o_ref, vp_ref, vs_ref, vb_s, t_s):
    i = pl.program_id(0)
    is_p = i < N_PTILES
    x = x_ref[...]
    h = _modulated(x, g_ref, modp_ref, mods_ref)
    z = _dot(h, win_ref[...]) + bin_ref[...]
    z = z / (1 + jnp.exp2(z * (GELU_A + GELU_B * (z * z))))
    v = z[:, GMLP_DIM:]
    mu = jnp.mean(v, axis=-1, keepdims=True)
    var = jnp.mean(jnp.square(v - mu), axis=-1, keepdims=True)
    vn = (v - mu) * lax.rsqrt(var + EPS) * lng_ref[...] + lnb_ref[...]

    @pl.when(jnp.logical_and(is_p, i % TILES_PER_SEQ == TILES_PER_SEQ - 1))
    def _():
        vp_ref[...] = vn[TM - CHUNK:]

    @pl.when(jnp.logical_not(is_p))
    def _():
        vs_ref[...] = vn

    vb_s[...] = vn.astype(BF16)
    r = lax.broadcasted_iota(jnp.int32, (CHUNK, CHUNK), 0)
    c = lax.broadcasted_iota(jnp.int32, (CHUNK, CHUNK), 1)
    same_seq = jnp.logical_or(is_p, jnp.bitwise_xor(r, c) < DEC_SEQ)
    keep = jnp.logical_and(r >= c, same_seq)
    bias = jnp.where(is_p, bp_ref[...], bs_ref[...])
    for g in range(GMLP_GROUPS):
        cols = slice(g * GROUP_DIM, (g + 1) * GROUP_DIM)
        w = jnp.where(keep, jnp.where(is_p, wmp_ref[g], wms_ref[g]), 0.0).astype(BF16)
        for ch in range(TM // CHUNK):
            rws = slice(ch * CHUNK, (ch + 1) * CHUNK)
            mixed = _dot(w, vb_s[rws, cols]) + bias[:, cols]
            t_s[rws, cols] = (z[rws, cols] * mixed).astype(BF16)
    y = _dot(t_s[...], wout_ref[...])
    o_ref[...] = _gated(x, _mod_vec(modp_ref, mods_ref, 2), y)


def _gmlp_sublayer(x, modp, mods, g, w_in, b_in, ln_g, ln_b, wm_p, wm_s, bias_p, bias_s, w_out):
    row = pl.BlockSpec((TM, D_MODEL), lambda i: (i, 0))
    return pl.pallas_call(
        _gmlp_kernel,
        grid=(N_TILES,),
        in_specs=[row] + _mod_specs() + [
            _resident((1, D_MODEL)),
            _resident(w_in.shape),
            _resident((1, 2 * GMLP_DIM)),
            _resident((1, GMLP_DIM)),
            _resident((1, GMLP_DIM)),
            _resident(wm_p.shape),
            _resident(wm_s.shape),
            _resident(bias_p.shape),
            _resident(bias_s.shape),
            _resident(w_out.shape),
        ],
        out_specs=[
            row,
            pl.BlockSpec((CHUNK, GMLP_DIM),
                         lambda i: (jnp.minimum(i // TILES_PER_SEQ, BATCH - 1), 0)),
            _srow_spec(GMLP_DIM),
        ],
        out_shape=[jax.ShapeDtypeStruct((ROWS, D_MODEL), F32),
                   jax.ShapeDtypeStruct((BATCH * CHUNK, GMLP_DIM), F32),
                   jax.ShapeDtypeStruct((S_ROWS, GMLP_DIM), F32)],
        scratch_shapes=[pltpu.VMEM((TM, GMLP_DIM), BF16), pltpu.VMEM((TM, GMLP_DIM), BF16)],
        compiler_params=_params(("arbitrary",)),
        name="gmlp_sublayer",
    )(x, modp, mods, g, w_in, b_in, ln_g, ln_b, wm_p, wm_s, bias_p, bias_s, w_out)


def _rope_table(pos):
    half = ROPE_DIM // 2
    freqs = jnp.power(ROPE_THETA, -jnp.arange(half, dtype=F32) / half)
    ang = pos[:, None] * freqs[None, :]
    cos, sin = jnp.cos(ang), jnp.sin(ang)
    return jnp.concatenate([cos, cos, -sin, sin], axis=-1)


def _with_rotated(w):
    half = ROPE_DIM // 2
    return jnp.concatenate([w, w[..., -half:], w[..., -ROPE_DIM:-half]], axis=-1)


def _split_mods(mod_l):
    out = []
    for k in range(N_SUB):
        m = mod_l[3 * k:3 * k + 3]
        out.append((m[:, :BATCH], m[:, BATCH:, None, :]))
    return out


def kernel(x_prompt, x_sample, cache_kv_latent, cache_k_rope, page_table, c_prompt, c_sample,
           ada_w, ada_b, norm_g, ffn_w_in, ffn_w_out,
           a_w_in, a_q_norm, a_w_uq, a_kv_norm, a_w_uk, a_w_uv, a_w_o,
           b_w_in, b_b_in, b_ln_g, b_ln_b, b_w_s, b_b_s, b_w_out, final_g):
    xs = (x_prompt.reshape(P_ROWS, D_MODEL), x_sample.reshape(S_ROWS, D_MODEL))
    c_all = jnp.concatenate([c_prompt, c_sample])
    mod = _ada_modulation(c_all, ada_w, ada_b)
    table_p = _rope_table(jnp.arange(SEQ, dtype=F32))
    table_s = _rope_table(jnp.tile(jnp.arange(DEC_SEQ, dtype=F32) + PAST_LEN, TM // DEC_SEQ))
    cache_rope_t = cache_k_rope.transpose(0, 1, 3, 2)

    lat_p, rope_p, lat_s, rope_s, v_p, v_s = [], [], [], [], [], []
    for i in range(DEPTH):
        mods_i = _split_mods(mod[i])
        j = i // 2
        x = _ffn_sublayer(xs if i == 0 else (x,), *mods_i[0], norm_g[i, 0][None], ffn_w_in, ffn_w_out, i, 0)
        if i % 2 == 0:
            w_in_ext = _with_rotated(a_w_in[j]).astype(BF16)
            w_uq_ext = _with_rotated(a_w_uq[j]).reshape(Q_RANK, N_HEADS * Q_HEAD_COLS).astype(BF16)
            w_uk_t = a_w_uk[j].transpose(1, 2, 0).astype(BF16)
            w_uk_flat = a_w_uk[j].reshape(KV_RANK, N_HEADS * NOPE_DIM).astype(BF16)
            w_uv_h = a_w_uv[j].transpose(1, 0, 2).astype(BF16)
            w_uv_t = a_w_uv[j].reshape(KV_RANK, N_HEADS * V_DIM).T.astype(BF16)
            l_p, r_p, l_s, r_s, q_p, k_p, v_t, q_s = _mla_project(
                x, *mods_i[1], norm_g[i, 1][None], w_in_ext, a_q_norm[j][None], w_uq_ext,
                a_kv_norm[j][None], w_uk_t, w_uk_flat, w_uv_t, table_p, table_s)
            o_p = _attn_prompt(q_p, k_p, v_t)
            q_s = q_s.reshape(N_HEADS, DEC_BATCH, DEC_SEQ, QK_DIM)
            q_s = q_s.transpose(1, 0, 2, 3).reshape(DEC_BATCH, N_HEADS * DEC_SEQ, QK_DIM)
            ctx_s = _attn_sample(j, page_table, q_s, l_s, r_s, cache_kv_latent, cache_rope_t)
            x = _mla_output(x, *mods_i[1], o_p, ctx_s.reshape(S_ROWS, N_HEADS * KV_RANK),
                            w_uv_h, a_w_o[j].astype(BF16))
            lat_p.append(l_p.reshape(BATCH, SEQ, KV_RANK))
            rope_p.append(r_p.reshape(BATCH, SEQ, ROPE_DIM))
            lat_s.append(l_s.reshape(DEC_BATCH, DEC_SEQ, KV_RANK))
            rope_s.append(r_s.reshape(DEC_BATCH, DEC_SEQ, ROPE_DIM))
        else:
            wm_s = jnp.tile(b_w_s[j][:, :DEC_SEQ, :DEC_SEQ], (1, CHUNK // DEC_SEQ, CHUNK // DEC_SEQ))
            bias_p = jnp.repeat(b_b_s[j].T, GROUP_DIM, axis=1)
            bias_s = jnp.tile(bias_p[:DEC_SEQ], (CHUNK // DEC_SEQ, 1))
            x, vp, vs = _gmlp_sublayer(
                x, *mods_i[1], norm_g[i, 1][None], b_w_in[j].astype(BF16), b_b_in[j][None],
                b_ln_g[j][None], b_ln_b[j][None], b_w_s[j], wm_s, bias_p, bias_s,
                b_w_out[j].astype(BF16))
            v_p.append(vp.reshape(BATCH, CHUNK, GMLP_DIM))
            v_s.append(vs.reshape(DEC_BATCH, DEC_SEQ, GMLP_DIM))
        x = _ffn_sublayer((x,), *mods_i[2], norm_g[i, 2][None], ffn_w_in, ffn_w_out, i, 1,
                          final_g=final_g[None] if i == DEPTH - 1 else None)

    y_p, y_s = x
    return (
        y_p.reshape(BATCH, SEQ, D_MODEL),
        y_s.reshape(DEC_BATCH, DEC_SEQ, D_MODEL),
        jnp.stack(lat_p), jnp.stack(rope_p), jnp.stack(lat_s), jnp.stack(rope_s),
        jnp.stack(v_p), jnp.stack(v_s),
    )
```

```python
import functools

import jax
import jax.numpy as jnp
from jax import lax
from jax.experimental import pallas as pl
from jax.experimental.pallas import tpu as pltpu

D_MODEL = 1024
BATCH = 8
SEQ = 2048
DEPTH = 4
DEC_BATCH = 128
DEC_SEQ = 8
PAST_LEN = 8192
PAGE_SIZE = 128
N_PAGES = PAST_LEN // PAGE_SIZE
N_SUB = 3
HALF_STEP = 0.5
EPS = 1e-6
N_HEADS = 8
NOPE_DIM = 128
ROPE_DIM = 64
V_DIM = 128
Q_RANK = 384
KV_RANK = 256
ROPE_THETA = 10000.0
ATTN_SCALE = (NOPE_DIM + ROPE_DIM) ** -0.5
LOG2_E = 1.4426950408889634
GELU_A = -2.0 * 0.7978845608028654 * LOG2_E
GELU_B = GELU_A * 0.044715
Q_SCALE = ATTN_SCALE * LOG2_E
CHUNK = 128
GMLP_DIM = D_MODEL
GMLP_GROUPS = 8
GROUP_DIM = GMLP_DIM // GMLP_GROUPS
FFN_DIM = 2816

P_ROWS = BATCH * SEQ
S_ROWS = DEC_BATCH * DEC_SEQ
ROWS = P_ROWS + S_ROWS
TM = 512
N_TILES = ROWS // TM
TILE_SEQS = TM // DEC_SEQ
N_PTILES = P_ROWS // TM
TILES_PER_SEQ = SEQ // TM
FFN_CHUNK = 256
N_FFN_CHUNKS = FFN_DIM // FFN_CHUNK
N_W_CHUNKS = 11
W_IN_COLS = 2 * FFN_DIM // N_W_CHUNKS
W_OUT_ROWS = FFN_DIM // N_W_CHUNKS
QK_DIM = KV_RANK + ROPE_DIM
HEAD_DIM = NOPE_DIM + ROPE_DIM
Q_HEAD_COLS = 2 * NOPE_DIM
TQ = 256
TK = 256
N_KBLOCKS = SEQ // TK
PAGE_SLOTS = 3
VMEM_LIMIT = 56 * 1024 * 1024

F32 = jnp.float32
BF16 = jnp.bfloat16


def _params(sem):
    return pltpu.CompilerParams(dimension_semantics=sem, vmem_limit_bytes=VMEM_LIMIT)


def _resident(shape):
    nd = len(shape)
    return pl.BlockSpec(shape, lambda *_: (0,) * nd, pipeline_mode=pl.Buffered(1))


def _ptile(i):
    return jnp.minimum(i, N_PTILES - 1)


def _stile(i):
    return jnp.maximum(i - N_PTILES, 0)


def _mod_specs():
    modp = _resident((3, BATCH, D_MODEL))
    mods = pl.BlockSpec((3, TILE_SEQS, 1, D_MODEL), lambda i: (0, _stile(i), 0, 0))
    return [modp, mods]


def _mod_vec(modp_ref, mods_ref, j):
    i = pl.program_id(0)
    b = jnp.minimum(i // TILES_PER_SEQ, BATCH - 1)
    return jnp.where(i < N_PTILES, modp_ref[j, pl.ds(b, 1), :][None], mods_ref[j])


def _per_seq(a):
    return a.reshape(TILE_SEQS, DEC_SEQ, a.shape[-1])


def _rms(x):
    return x * lax.rsqrt(jnp.mean(x * x, axis=-1, keepdims=True) + EPS)


def _modulated(x, g_ref, modp_ref, mods_ref):
    shift = _mod_vec(modp_ref, mods_ref, 0)
    scale = _mod_vec(modp_ref, mods_ref, 1)
    y = _per_seq(_rms(x) * g_ref[...]) * (1 + scale) + shift
    return y.reshape(x.shape).astype(BF16)


def _gated(x, gate, y):
    return (_per_seq(x) + gate * _per_seq(y)).reshape(x.shape)


def _dot(a, b):
    return jnp.dot(a, b, preferred_element_type=F32)


def _dot_nt(a, b):
    return lax.dot_general(a, b, (((1,), (1,)), ((), ())), preferred_element_type=F32)


def _mod_kernel(c_ref, w_ref, b_ref, o_ref):
    c = c_ref[...]
    a = (c * jax.nn.sigmoid(c)).astype(BF16)
    for t in range(3):
        cols = slice(t * D_MODEL, (t + 1) * D_MODEL)
        o_ref[0, t] = _dot(a, w_ref[0, :, cols].astype(BF16)) + b_ref[0, :, cols]


def _ada_modulation(c_all, ada_w, ada_b):
    n = c_all.shape[0]
    width = N_SUB * 3 * D_MODEL
    return pl.pallas_call(
        _mod_kernel,
        grid=(DEPTH, N_SUB),
        in_specs=[
            pl.BlockSpec((n, D_MODEL), lambda l, j: (0, 0)),
            pl.BlockSpec((1, D_MODEL, 3 * D_MODEL), lambda l, j: (l, 0, j)),
            pl.BlockSpec((1, 1, 3 * D_MODEL), lambda l, j: (l, 0, j)),
        ],
        out_specs=pl.BlockSpec((1, 3, n, D_MODEL), lambda l, j: (l, j, 0, 0)),
        out_shape=jax.ShapeDtypeStruct((DEPTH, N_SUB * 3, n, D_MODEL), F32),
        compiler_params=_params(("arbitrary", "arbitrary")),
        name="ada_modulation",
    )(c_all, ada_w, ada_b.reshape(DEPTH, 1, width))


def _ffn_kernel(n_x, final, layer, k, *refs):
    x_refs, refs = refs[:n_x], refs[n_x:]
    modp_ref, mods_ref, g_ref, win_hbm, wo_hbm = refs[:5]
    refs = refs[5:]
    h_s, a_s, win_s, wo_s, stin_s, sto_s, sem = refs[-7:]
    is_p = pl.program_id(0) < N_PTILES

    @pl.when(pl.program_id(0) == 0)
    def _():
        def in_copy(c):
            return pltpu.make_async_copy(
                win_hbm.at[layer, k, :, pl.ds(c * W_IN_COLS, W_IN_COLS)],
                stin_s.at[c % 2], sem.at[0, c % 2])

        def out_copy(c):
            return pltpu.make_async_copy(
                wo_hbm.at[layer, k, pl.ds(c * W_OUT_ROWS, W_OUT_ROWS), :],
                sto_s.at[c % 2], sem.at[1, c % 2])

        in_copy(0).start()
        out_copy(0).start()
        for c in range(N_W_CHUNKS):
            if c + 1 < N_W_CHUNKS:
                in_copy(c + 1).start()
                out_copy(c + 1).start()
            in_copy(c).wait()
            win_s[:, c * W_IN_COLS:(c + 1) * W_IN_COLS] = stin_s[c % 2].astype(BF16)
            out_copy(c).wait()
            wo_s[c * W_OUT_ROWS:(c + 1) * W_OUT_ROWS, :] = sto_s[c % 2].astype(BF16)

    x = x_refs[0][...] if n_x == 1 else jnp.where(is_p, x_refs[0][...], x_refs[1][...])
    h_s[...] = _modulated(x, g_ref, modp_ref, mods_ref)
    for c in range(N_FFN_CHUNKS):
        h = h_s[...]
        gate = _dot(h, win_s[:, c * FFN_CHUNK:(c + 1) * FFN_CHUNK])
        up = _dot(h, win_s[:, FFN_DIM + c * FFN_CHUNK:FFN_DIM + (c + 1) * FFN_CHUNK])
        a_s[:, c * FFN_CHUNK:(c + 1) * FFN_CHUNK] = (gate * jax.nn.sigmoid(gate) * up).astype(BF16)
    y = _dot(a_s[...], wo_s[...])
    out = _gated(x, HALF_STEP * _mod_vec(modp_ref, mods_ref, 2), y)
    if not final:
        refs[0][...] = out
        return
    fg_ref, yp_ref, ys_ref = refs[:3]
    normed = _rms(out) * fg_ref[...]

    @pl.when(is_p)
    def _():
        yp_ref[...] = normed

    @pl.when(jnp.logical_not(is_p))
    def _():
        ys_ref[...] = normed


def _row_spec(width=D_MODEL):
    return pl.BlockSpec((TM, width), lambda i: (i, 0))


def _prow_spec(width=D_MODEL):
    return pl.BlockSpec((TM, width), lambda i: (_ptile(i), 0))


def _srow_spec(width=D_MODEL):
    return pl.BlockSpec((TM, width), lambda i: (_stile(i), 0))


def _ffn_sublayer(xs, modp, mods, g, w_in, w_out, layer, k, final_g=None):
    final = final_g is not None
    x_specs = [_row_spec()] if len(xs) == 1 else [_prow_spec(), _srow_spec()]
    in_specs = x_specs + _mod_specs() + [
        _resident((1, D_MODEL)),
        pl.BlockSpec(memory_space=pl.ANY),
        pl.BlockSpec(memory_space=pl.ANY),
    ]
    args = list(xs) + [modp, mods, g, w_in, w_out]
    if final:
        in_specs.append(_resident((1, D_MODEL)))
        args.append(final_g)
        out_specs = [_prow_spec(), _srow_spec()]
        out_shape = [jax.ShapeDtypeStruct((P_ROWS, D_MODEL), F32),
                     jax.ShapeDtypeStruct((S_ROWS, D_MODEL), F32)]
    else:
        out_specs = _row_spec()
        out_shape = jax.ShapeDtypeStruct((ROWS, D_MODEL), F32)
    return pl.pallas_call(
        functools.partial(_ffn_kernel, len(xs), final, layer, k),
        grid=(N_TILES,),
        in_specs=in_specs,
        out_specs=out_specs,
        out_shape=out_shape,
        scratch_shapes=[
            pltpu.VMEM((TM, D_MODEL), BF16), pltpu.VMEM((TM, FFN_DIM), BF16),
            pltpu.VMEM((D_MODEL, 2 * FFN_DIM), BF16), pltpu.VMEM((FFN_DIM, D_MODEL), BF16),
            pltpu.VMEM((2, D_MODEL, W_IN_COLS), F32), pltpu.VMEM((2, W_OUT_ROWS, D_MODEL), F32),
            pltpu.SemaphoreType.DMA((2, 2)),
        ],
        compiler_params=_params(("arbitrary",)),
        name="ffn_sublayer",
    )(*args)


def _rope(t, table):
    w = t * table
    return w + pltpu.roll(w, ROPE_DIM, axis=1)


def _mla_proj_kernel(x_ref, modp_ref, mods_ref, g_ref, win_ref, qn_ref, wuq_ref, kvn_ref,
                     wukt_ref, wuk_ref, wuvt_ref, tabp_ref, tabs_ref,
                     latp_ref, krp_ref, lats_ref, krs_ref, qp_ref, kp_ref, vt_ref, qs_ref):
    is_p = pl.program_id(0) < N_PTILES
    h = _modulated(x_ref[...], g_ref, modp_ref, mods_ref)
    z = _dot(h, win_ref[...])
    table = jnp.where(is_p, tabp_ref[...], tabs_ref[...])
    k_rope = _rope(z[:, Q_RANK + KV_RANK:], table)[:, :ROPE_DIM]
    latent = _rms(z[:, Q_RANK:Q_RANK + KV_RANK]) * kvn_ref[...]
    qn = (_rms(z[:, :Q_RANK]) * qn_ref[...]).astype(BF16)
    q_all = _dot(qn, wuq_ref[...])

    def q_rope(hd):
        t = q_all[:, hd * Q_HEAD_COLS + NOPE_DIM:(hd + 1) * Q_HEAD_COLS]
        return (_rope(t, table)[:, :ROPE_DIM] * Q_SCALE).astype(BF16)

    @pl.when(is_p)
    def _():
        latp_ref[...] = latent
        krp_ref[...] = k_rope
        lat_b = latent.astype(BF16)
        k_rope_b = k_rope.astype(BF16)
        k_nope = _dot(lat_b, wuk_ref[...])
        v_t = _dot_nt(wuvt_ref[...], lat_b).astype(BF16)
        for kb in range(TM // TK):
            vt_ref[kb] = v_t[:, kb * TK:(kb + 1) * TK]
        for hd in range(N_HEADS):
            q_nope = q_all[:, hd * Q_HEAD_COLS:hd * Q_HEAD_COLS + NOPE_DIM]
            qp_ref[hd, :, :NOPE_DIM] = (q_nope * Q_SCALE).astype(BF16)
            qp_ref[hd, :, NOPE_DIM:] = q_rope(hd)
            kp_ref[hd, :, :NOPE_DIM] = k_nope[:, hd * NOPE_DIM:(hd + 1) * NOPE_DIM].astype(BF16)
            kp_ref[hd, :, NOPE_DIM:] = k_rope_b

    @pl.when(jnp.logical_not(is_p))
    def _():
        lats_ref[...] = latent
        krs_ref[...] = k_rope
        for hd in range(N_HEADS):
            q_nope = q_all[:, hd * Q_HEAD_COLS:hd * Q_HEAD_COLS + NOPE_DIM].astype(BF16)
            q_lat = _dot(q_nope, wukt_ref[hd]) * Q_SCALE
            qs_ref[hd, :, :KV_RANK] = q_lat.astype(BF16)
            qs_ref[hd, :, KV_RANK:] = q_rope(hd)


def _mla_project(x, modp, mods, g, w_in_ext, q_norm, w_uq_ext, kv_norm, w_uk_t, w_uk_flat,
                 w_uv_t, table_p, table_s):
    return pl.pallas_call(
        _mla_proj_kernel,
        grid=(N_TILES,),
        in_specs=[_row_spec()] + _mod_specs() + [
            _resident((1, D_MODEL)),
            _resident(w_in_ext.shape),
            _resident((1, Q_RANK)),
            _resident(w_uq_ext.shape),
            _resident((1, KV_RANK)),
            _resident(w_uk_t.shape),
            _resident(w_uk_flat.shape),
            _resident(w_uv_t.shape),
            pl.BlockSpec((TM, 2 * ROPE_DIM), lambda i: (i % TILES_PER_SEQ, 0)),
            _resident((TM, 2 * ROPE_DIM)),
        ],
        out_specs=[
            _prow_spec(KV_RANK),
            _prow_spec(ROPE_DIM),
            _srow_spec(KV_RANK),
            _srow_spec(ROPE_DIM),
            pl.BlockSpec((N_HEADS, TM, HEAD_DIM), lambda i: (0, _ptile(i), 0)),
            pl.BlockSpec((N_HEADS, TM, HEAD_DIM), lambda i: (0, _ptile(i), 0)),
            pl.BlockSpec((TM // TK, N_HEADS * V_DIM, TK), lambda i: (_ptile(i), 0, 0)),
            pl.BlockSpec((N_HEADS, TM, QK_DIM), lambda i: (0, _stile(i), 0)),
        ],
        out_shape=[
            jax.ShapeDtypeStruct((P_ROWS, KV_RANK), F32),
            jax.ShapeDtypeStruct((P_ROWS, ROPE_DIM), F32),
            jax.ShapeDtypeStruct((S_ROWS, KV_RANK), F32),
            jax.ShapeDtypeStruct((S_ROWS, ROPE_DIM), F32),
            jax.ShapeDtypeStruct((N_HEADS, P_ROWS, HEAD_DIM), BF16),
            jax.ShapeDtypeStruct((N_HEADS, P_ROWS, HEAD_DIM), BF16),
            jax.ShapeDtypeStruct((P_ROWS // TK, N_HEADS * V_DIM, TK), BF16),
            jax.ShapeDtypeStruct((N_HEADS, S_ROWS, QK_DIM), BF16),
        ],
        compiler_params=_params(("arbitrary",)),
        name="mla_project",
    )(x, modp, mods, g, w_in_ext, q_norm, w_uq_ext, kv_norm, w_uk_t, w_uk_flat, w_uv_t,
      table_p, table_s)


def _attn_prompt_kernel(q_ref, k_ref, vt_ref, o_ref, m_s, l_s, acc_s, a_s, p_s):
    qi = pl.program_id(1)
    m_s[...] = jnp.full_like(m_s, -jnp.inf)
    l_s[...] = jnp.zeros_like(l_s)
    acc_s[...] = jnp.zeros_like(acc_s)

    def step(kb0, n_kb, masked):
        keys = n_kb * TK
        off = pl.multiple_of(kb0 * TK, TK)
        for hd in range(N_HEADS):
            s = _dot_nt(k_ref[hd, pl.ds(off, keys), :], q_ref[hd])
            if masked:
                k_pos = off + lax.broadcasted_iota(jnp.int32, (keys, TQ), 0)
                q_pos = qi * TQ + lax.broadcasted_iota(jnp.int32, (keys, TQ), 1)
                s = jnp.where(k_pos <= q_pos, s, -jnp.inf)
            m_old = m_s[hd]
            m_new = jnp.maximum(m_old, jnp.max(s, axis=0, keepdims=True))
            alpha = jnp.exp2(m_old - m_new)
            p = jnp.exp2(s - m_new)
            l_s[hd] = alpha * l_s[hd] + jnp.sum(p, axis=0, keepdims=True)
            p_s[hd, :keys] = p.astype(BF16)
            a_s[hd] = alpha
            m_s[hd] = m_new
        for hd in range(N_HEADS):
            pv = _dot(vt_ref[kb0, hd * V_DIM:(hd + 1) * V_DIM, :], p_s[hd, :TK])
            for j in range(1, n_kb):
                pv += _dot(vt_ref[kb0 + j, hd * V_DIM:(hd + 1) * V_DIM, :],
                           p_s[hd, j * TK:(j + 1) * TK])
            acc_s[hd] = a_s[hd] * acc_s[hd] + pv

    def body(j, carry):
        step(2 * j, 2, False)
        return carry

    lax.fori_loop(0, qi // 2, body, 0)

    @pl.when(qi % 2 == 1)
    def _():
        step(qi - 1, 2, True)

    @pl.when(qi % 2 == 0)
    def _():
        step(qi, 1, True)
    for hd in range(N_HEADS):
        o_t = acc_s[hd] / l_s[hd]
        o_ref[:, hd * V_DIM:(hd + 1) * V_DIM] = o_t.T.astype(BF16)


def _attn_prompt(q_p, k_p, v_t):
    nq = SEQ // TQ
    return pl.pallas_call(
        _attn_prompt_kernel,
        grid=(BATCH, nq),
        in_specs=[
            pl.BlockSpec((N_HEADS, TQ, HEAD_DIM), lambda b, i: (0, b * nq + i, 0)),
            pl.BlockSpec((N_HEADS, SEQ, HEAD_DIM), lambda b, i: (0, b, 0)),
            pl.BlockSpec((N_KBLOCKS, N_HEADS * V_DIM, TK), lambda b, i: (b, 0, 0)),
        ],
        out_specs=pl.BlockSpec((TQ, N_HEADS * V_DIM), lambda b, i: (b * nq + i, 0)),
        out_shape=jax.ShapeDtypeStruct((P_ROWS, N_HEADS * V_DIM), BF16),
        scratch_shapes=[pltpu.VMEM((N_HEADS, 1, TQ), F32), pltpu.VMEM((N_HEADS, 1, TQ), F32),
                        pltpu.VMEM((N_HEADS, V_DIM, TQ), F32), pltpu.VMEM((N_HEADS, 1, TQ), F32),
                        pltpu.VMEM((N_HEADS, 2 * TK, TQ), BF16)],
        compiler_params=_params(("arbitrary", "arbitrary")),
        name="attn_prompt",
    )(q_p, k_p, v_t)


def _attn_sample_kernel(layer, pt_ref, q_ref, ln_ref, rn_ref, lat_hbm, rope_hbm, o_ref,
                        lat_buf, rope_buf, kl_s, kr_s, sem):
    b = pl.program_id(0)
    slot = lax.rem(b, PAGE_SLOTS)
    rows = N_HEADS * DEC_SEQ

    def page_copies(page, sl, p):
        return (pltpu.make_async_copy(lat_hbm.at[layer, page], lat_buf.at[sl, p], sem.at[sl, 0]),
                pltpu.make_async_copy(rope_hbm.at[layer, page], rope_buf.at[sl, p], sem.at[sl, 1]))

    def start_pages(seq, sl):
        def body(p, carry):
            for cp in page_copies(pt_ref[seq, p], sl, p):
                cp.start()
            return carry
        lax.fori_loop(0, N_PAGES, body, 0, unroll=True)

    @pl.when(b == 0)
    def _():
        for ahead in range(PAGE_SLOTS - 1):
            start_pages(ahead, ahead)

    @pl.when(b + PAGE_SLOTS - 1 < DEC_BATCH)
    def _():
        start_pages(b + PAGE_SLOTS - 1, lax.rem(b + PAGE_SLOTS - 1, PAGE_SLOTS))

    for p in range(N_PAGES):
        for cp in page_copies(0, slot, p):
            cp.wait()

    for p in range(N_PAGES):
        kl_s[p * PAGE_SIZE:(p + 1) * PAGE_SIZE, :] = lat_buf[slot, p].astype(BF16)
        kr_s[:, p * PAGE_SIZE:(p + 1) * PAGE_SIZE] = rope_buf[slot, p].astype(BF16)

    q = q_ref[0]
    q_lat = q[:, :KV_RANK]
    q_rope = q[:, KV_RANK:]
    k_lat = kl_s[...]
    s = _dot_nt(q_lat, k_lat) + _dot(q_rope, kr_s[...])
    new_lat = ln_ref[...].astype(BF16)
    new_rope = rn_ref[...].astype(BF16)
    s_new = _dot_nt(q_lat, new_lat) + _dot_nt(q_rope, new_rope)
    q_pos = lax.broadcasted_iota(jnp.int32, (rows, DEC_SEQ), 0) & (DEC_SEQ - 1)
    k_pos = lax.broadcasted_iota(jnp.int32, (rows, DEC_SEQ), 1)
    s_new = jnp.where(k_pos <= q_pos, s_new, -jnp.inf)
    m = jnp.maximum(jnp.max(s, axis=-1, keepdims=True), jnp.max(s_new, axis=-1, keepdims=True))
    p = jnp.exp2(s - m)
    p_new = jnp.exp2(s_new - m)
    l = jnp.sum(p, axis=-1, keepdims=True) + jnp.sum(p_new, axis=-1, keepdims=True)
    pb = p.astype(BF16)
    half = KV_RANK // 2
    ctx = jnp.concatenate([_dot(pb, kl_s[:, :half]), _dot(pb, kl_s[:, half:])], axis=1)
    ctx = (ctx + _dot(p_new.astype(BF16), new_lat)) / l
    for hd in range(N_HEADS):
        o_ref[0, :, hd * KV_RANK:(hd + 1) * KV_RANK] = ctx[hd * DEC_SEQ:(hd + 1) * DEC_SEQ]


def _attn_sample(layer, page_table, q_s, lat, k_rope, cache_lat, cache_rope_t):
    rows = N_HEADS * DEC_SEQ
    return pl.pallas_call(
        functools.partial(_attn_sample_kernel, layer),
        grid_spec=pltpu.PrefetchScalarGridSpec(
            num_scalar_prefetch=1,
            grid=(DEC_BATCH,),
            in_specs=[
                pl.BlockSpec((1, rows, QK_DIM), lambda b, pt: (b, 0, 0)),
                pl.BlockSpec((DEC_SEQ, KV_RANK), lambda b, pt: (b, 0)),
                pl.BlockSpec((DEC_SEQ, ROPE_DIM), lambda b, pt: (b, 0)),
                pl.BlockSpec(memory_space=pl.ANY),
                pl.BlockSpec(memory_space=pl.ANY),
            ],
            out_specs=pl.BlockSpec((1, DEC_SEQ, N_HEADS * KV_RANK), lambda b, pt: (b, 0, 0)),
            scratch_shapes=[
                pltpu.VMEM((PAGE_SLOTS, N_PAGES, PAGE_SIZE, KV_RANK), F32),
                pltpu.VMEM((PAGE_SLOTS, N_PAGES, ROPE_DIM, PAGE_SIZE), F32),
                pltpu.VMEM((PAST_LEN, KV_RANK), BF16),
                pltpu.VMEM((ROPE_DIM, PAST_LEN), BF16),
                pltpu.SemaphoreType.DMA((PAGE_SLOTS, 2)),
            ],
        ),
        out_shape=jax.ShapeDtypeStruct((DEC_BATCH, DEC_SEQ, N_HEADS * KV_RANK), F32),
        compiler_params=_params(("arbitrary",)),
        name="attn_sample",
    )(page_table, q_s, lat, k_rope, cache_lat, cache_rope_t)


def _mla_out_kernel(x_ref, modp_ref, mods_ref, op_ref, cs_ref, wuv_ref, wo_ref, o_ref, o_s):
    is_p = pl.program_id(0) < N_PTILES

    @pl.when(is_p)
    def _():
        o_s[...] = op_ref[...]

    @pl.when(jnp.logical_not(is_p))
    def _():
        for hd in range(N_HEADS):
            ctx = cs_ref[:, hd * KV_RANK:(hd + 1) * KV_RANK].astype(BF16)
            o_s[:, hd * V_DIM:(hd + 1) * V_DIM] = _dot(ctx, wuv_ref[hd]).astype(BF16)

    y = _dot(o_s[...], wo_ref[...])
    o_ref[...] = _gated(x_ref[...], _mod_vec(modp_ref, mods_ref, 2), y)


def _mla_output(x, modp, mods, o_p, ctx_s, w_uv_h, w_o):
    row = pl.BlockSpec((TM, D_MODEL), lambda i: (i, 0))
    return pl.pallas_call(
        _mla_out_kernel,
        grid=(N_TILES,),
        in_specs=[row] + _mod_specs() + [
            pl.BlockSpec((TM, N_HEADS * V_DIM), lambda i: (_ptile(i), 0)),
            pl.BlockSpec((TM, N_HEADS * KV_RANK), lambda i: (_stile(i), 0)),
            _resident(w_uv_h.shape),
            _resident(w_o.shape),
        ],
        out_specs=row,
        out_shape=jax.ShapeDtypeStruct((ROWS, D_MODEL), F32),
        scratch_shapes=[pltpu.VMEM((TM, N_HEADS * V_DIM), BF16)],
        compiler_params=_params(("arbitrary",)),
        name="mla_output",
    )(x, modp, mods, o_p, ctx_s, w_uv_h, w_o)


def _gmlp_kernel(x_ref, modp_ref, mods_ref, g_ref, win_ref, bin_ref, lng_ref, lnb_ref,
                 wmp_ref, wms_ref, bp_ref, bs_ref, wout_ref, o_ref, vp_ref, vs_ref, vb_s, t_s):
    i = pl.program_id(0)
    is_p = i < N_PTILES
    x = x_ref[...]
    h = _modulated(x, g_ref, modp_ref, mods_ref)
    z = _dot(h, win_ref[...]) + bin_ref[...]
    z = z / (1 + jnp.exp2(z * (GELU_A + GELU_B * (z * z))))
    v = z[:, GMLP_DIM:]
    mu = jnp.mean(v, axis=-1, keepdims=True)
    var = jnp.mean(jnp.square(v - mu), axis=-1, keepdims=True)
    vn = (v - mu) * lax.rsqrt(var + EPS) * lng_ref[...] + lnb_ref[...]

    @pl.when(jnp.logical_and(is_p, i % TILES_PER_SEQ == TILES_PER_SEQ - 1))
    def _():
        vp_ref[...] = vn[TM - CHUNK:]

    @pl.when(jnp.logical_not(is_p))
    def _():
        vs_ref[...] = vn

    vb_s[...] = vn.astype(BF16)
    r = lax.broadcasted_iota(jnp.int32, (CHUNK, CHUNK), 0)
    c = lax.broadcasted_iota(jnp.int32, (CHUNK, CHUNK), 1)
    same_seq = jnp.logical_or(is_p, jnp.bitwise_xor(r, c) < DEC_SEQ)
    keep = jnp.logical_and(r >= c, same_seq)
    bias = jnp.where(is_p, bp_ref[...], bs_ref[...])
    for g in range(GMLP_GROUPS):
        cols = slice(g * GROUP_DIM, (g + 1) * GROUP_DIM)
        w = jnp.where(keep, jnp.where(is_p, wmp_ref[g], wms_ref[g]), 0.0).astype(BF16)
        n_ch = TM // CHUNK
        v_g = jnp.concatenate([vb_s[ch * CHUNK:(ch + 1) * CHUNK, cols] for ch in range(n_ch)], axis=1)
        mixed = _dot(w, v_g)
        for ch in range(n_ch):
            rws = slice(ch * CHUNK, (ch + 1) * CHUNK)
            m_ch = mixed[:, ch * GROUP_DIM:(ch + 1) * GROUP_DIM] + bias[:, cols]
            t_s[rws, cols] = (z[rws, cols] * m_ch).astype(BF16)
    y = _dot(t_s[...], wout_ref[...])
    o_ref[...] = _gated(x, _mod_vec(modp_ref, mods_ref, 2), y)


def _gmlp_sublayer(x, modp, mods, g, w_in, b_in, ln_g, ln_b, wm_p, wm_s, bias_p, bias_s, w_out):
    row = pl.BlockSpec((TM, D_MODEL), lambda i: (i, 0))
    return pl.pallas_call(
        _gmlp_kernel,
        grid=(N_TILES,),
        in_specs=[row] + _mod_specs() + [
            _resident((1, D_MODEL)),
            _resident(w_in.shape),
            _resident((1, 2 * GMLP_DIM)),
            _resident((1, GMLP_DIM)),
            _resident((1, GMLP_DIM)),
            _resident(wm_p.shape),
            _resident(wm_s.shape),
            _resident(bias_p.shape),
            _resident(bias_s.shape),
            _resident(w_out.shape),
        ],
        out_specs=[
            row,
            pl.BlockSpec((CHUNK, GMLP_DIM),
                         lambda i: (jnp.minimum(i // TILES_PER_SEQ, BATCH - 1), 0)),
            _srow_spec(GMLP_DIM),
        ],
        out_shape=[jax.ShapeDtypeStruct((ROWS, D_MODEL), F32),
                   jax.ShapeDtypeStruct((BATCH * CHUNK, GMLP_DIM), F32),
                   jax.ShapeDtypeStruct((S_ROWS, GMLP_DIM), F32)],
        scratch_shapes=[pltpu.VMEM((TM, GMLP_DIM), BF16), pltpu.VMEM((TM, GMLP_DIM), BF16)],
        compiler_params=_params(("arbitrary",)),
        name="gmlp_sublayer",
    )(x, modp, mods, g, w_in, b_in, ln_g, ln_b, wm_p, wm_s, bias_p, bias_s, w_out)


def _rope_table(pos):
    half = ROPE_DIM // 2
    freqs = jnp.power(ROPE_THETA, -jnp.arange(half, dtype=F32) / half)
    ang = pos[:, None] * freqs[None, :]
    cos, sin = jnp.cos(ang), jnp.sin(ang)
    return jnp.concatenate([cos, cos, -sin, sin], axis=-1)


def _with_rotated(w):
    half = ROPE_DIM // 2
    return jnp.concatenate([w, w[..., -half:], w[..., -ROPE_DIM:-half]], axis=-1)


def _split_mods(mod_l):
    out = []
    for k in range(N_SUB):
        m = mod_l[3 * k:3 * k + 3]
        out.append((m[:, :BATCH], m[:, BATCH:, None, :]))
    return out


def kernel(x_prompt, x_sample, cache_kv_latent, cache_k_rope, page_table, c_prompt, c_sample,
           ada_w, ada_b, norm_g, ffn_w_in, ffn_w_out,
           a_w_in, a_q_norm, a_w_uq, a_kv_norm, a_w_uk, a_w_uv, a_w_o,
           b_w_in, b_b_in, b_ln_g, b_ln_b, b_w_s, b_b_s, b_w_out, final_g):
    xs = (x_prompt.reshape(P_ROWS, D_MODEL), x_sample.reshape(S_ROWS, D_MODEL))
    c_all = jnp.concatenate([c_prompt, c_sample])
    mod = _ada_modulation(c_all, ada_w, ada_b)
    table_p = _rope_table(jnp.arange(SEQ, dtype=F32))
    table_s = _rope_table(jnp.tile(jnp.arange(DEC_SEQ, dtype=F32) + PAST_LEN, TM // DEC_SEQ))
    cache_rope_t = cache_k_rope.transpose(0, 1, 3, 2)

    lat_p, rope_p, lat_s, rope_s, v_p, v_s = [], [], [], [], [], []
    for i in range(DEPTH):
        mods_i = _split_mods(mod[i])
        j = i // 2
        x = _ffn_sublayer(xs if i == 0 else (x,), *mods_i[0], norm_g[i, 0][None], ffn_w_in, ffn_w_out, i, 0)
        if i % 2 == 0:
            w_in_ext = _with_rotated(a_w_in[j]).astype(BF16)
            w_uq_ext = _with_rotated(a_w_uq[j]).reshape(Q_RANK, N_HEADS * Q_HEAD_COLS).astype(BF16)
            w_uk_t = a_w_uk[j].transpose(1, 2, 0).astype(BF16)
            w_uk_flat = a_w_uk[j].reshape(KV_RANK, N_HEADS * NOPE_DIM).astype(BF16)
            w_uv_h = a_w_uv[j].transpose(1, 0, 2).astype(BF16)
            w_uv_t = a_w_uv[j].reshape(KV_RANK, N_HEADS * V_DIM).T.astype(BF16)
            l_p, r_p, l_s, r_s, q_p, k_p, v_t, q_s = _mla_project(
                x, *mods_i[1], norm_g[i, 1][None], w_in_ext, a_q_norm[j][None], w_uq_ext,
                a_kv_norm[j][None], w_uk_t, w_uk_flat, w_uv_t, table_p, table_s)
            o_p = _attn_prompt(q_p, k_p, v_t)
            q_s = q_s.reshape(N_HEADS, DEC_BATCH, DEC_SEQ, QK_DIM)
            q_s = q_s.transpose(1, 0, 2, 3).reshape(DEC_BATCH, N_HEADS * DEC_SEQ, QK_DIM)
            ctx_s = _attn_sample(j, page_table, q_s, l_s, r_s, cache_kv_latent, cache_rope_t)
            x = _mla_output(x, *mods_i[1], o_p, ctx_s.reshape(S_ROWS, N_HEADS * KV_RANK),
                            w_uv_h, a_w_o[j].astype(BF16))
            lat_p.append(l_p.reshape(BATCH, SEQ, KV_RANK))
            rope_p.append(r_p.reshape(BATCH, SEQ, ROPE_DIM))
            lat_s.append(l_s.reshape(DEC_BATCH, DEC_SEQ, KV_RANK))
            rope_s.append(r_s.reshape(DEC_BATCH, DEC_SEQ, ROPE_DIM))
        else:
            wm_s = jnp.tile(b_w_s[j][:, :DEC_SEQ, :DEC_SEQ], (1, CHUNK // DEC_SEQ, CHUNK // DEC_SEQ))
            bias_p = jnp.repeat(b_b_s[j].T, GROUP_DIM, axis=1)
            bias_s = jnp.tile(bias_p[:DEC_SEQ], (CHUNK // DEC_SEQ, 1))
            x, vp, vs = _gmlp_sublayer(
                x, *mods_i[1], norm_g[i, 1][None], b_w_in[j].astype(BF16), b_b_in[j][None],
                b_ln_g[j][None], b_ln_b[j][None], b_w_s[j], wm_s, bias_p, bias_s,
                b_w_out[j].astype(BF16))
            v_p.append(vp.reshape(BATCH, CHUNK, GMLP_DIM))
            v_s.append(vs.reshape(DEC_BATCH, DEC_SEQ, GMLP_DIM))
        x = _ffn_sublayer((x,), *mods_i[2], norm_g[i, 2][None], ffn_w_in, ffn_w_out, i, 1,
                          final_g=final_g[None] if i == DEPTH - 1 else None)

    y_p, y_s = x
    return (
        y_p.reshape(BATCH, SEQ, D_MODEL),
        y_s.reshape(DEC_BATCH, DEC_SEQ, D_MODEL),
        jnp.stack(lat_p), jnp.stack(rope_p), jnp.stack(lat_s), jnp.stack(rope_s),
        jnp.stack(v_p), jnp.stack(v_s),
    )
```

```python
import functools

import jax
import jax.numpy as jnp
from jax import lax
from jax.experimental import pallas as pl
from jax.experimental.pallas import tpu as pltpu

D_MODEL = 1024
BATCH = 8
SEQ = 2048
DEPTH = 4
DEC_BATCH = 128
DEC_SEQ = 8
PAST_LEN = 8192
PAGE_SIZE = 128
N_PAGES = PAST_LEN // PAGE_SIZE
N_SUB = 3
HALF_STEP = 0.5
EPS = 1e-6
N_HEADS = 8
NOPE_DIM = 128
ROPE_DIM = 64
V_DIM = 128
Q_RANK = 384
KV_RANK = 256
ROPE_THETA = 10000.0
ATTN_SCALE = (NOPE_DIM + ROPE_DIM) ** -0.5
LOG2_E = 1.4426950408889634
GELU_A = -2.0 * 0.7978845608028654 * LOG2_E
GELU_B = GELU_A * 0.044715
Q_SCALE = ATTN_SCALE * LOG2_E
CHUNK = 128
GMLP_DIM = D_MODEL
GMLP_GROUPS = 8
GROUP_DIM = GMLP_DIM // GMLP_GROUPS
FFN_DIM = 2816

P_ROWS = BATCH * SEQ
S_ROWS = DEC_BATCH * DEC_SEQ
ROWS = P_ROWS + S_ROWS
TM = 512
N_TILES = ROWS // TM
TILE_SEQS = TM // DEC_SEQ
N_PTILES = P_ROWS // TM
TILES_PER_SEQ = SEQ // TM
FFN_CHUNK = 256
N_FFN_CHUNKS = FFN_DIM // FFN_CHUNK
N_W_CHUNKS = 11
W_IN_COLS = 2 * FFN_DIM // N_W_CHUNKS
W_OUT_ROWS = FFN_DIM // N_W_CHUNKS
QK_DIM = KV_RANK + ROPE_DIM
HEAD_DIM = NOPE_DIM + ROPE_DIM
Q_HEAD_COLS = 2 * NOPE_DIM
TQ = 256
TK = 256
N_KBLOCKS = SEQ // TK
PAGE_SLOTS = 3
VMEM_LIMIT = 56 * 1024 * 1024

F32 = jnp.float32
BF16 = jnp.bfloat16


def _params(sem):
    return pltpu.CompilerParams(dimension_semantics=sem, vmem_limit_bytes=VMEM_LIMIT)


def _resident(shape):
    nd = len(shape)
    return pl.BlockSpec(shape, lambda *_: (0,) * nd, pipeline_mode=pl.Buffered(1))


def _ptile(i):
    return jnp.minimum(i, N_PTILES - 1)


def _stile(i):
    return jnp.maximum(i - N_PTILES, 0)


def _mod_specs():
    modp = _resident((3, BATCH, D_MODEL))
    mods = pl.BlockSpec((3, TILE_SEQS, 1, D_MODEL), lambda i: (0, _stile(i), 0, 0))
    return [modp, mods]


def _mod_vec(modp_ref, mods_ref, j):
    i = pl.program_id(0)
    b = jnp.minimum(i // TILES_PER_SEQ, BATCH - 1)
    return jnp.where(i < N_PTILES, modp_ref[j, pl.ds(b, 1), :][None], mods_ref[j])


def _per_seq(a):
    return a.reshape(TILE_SEQS, DEC_SEQ, a.shape[-1])


def _rms(x):
    return x * lax.rsqrt(jnp.mean(x * x, axis=-1, keepdims=True) + EPS)


def _modulated(x, g_ref, modp_ref, mods_ref):
    shift = _mod_vec(modp_ref, mods_ref, 0)
    scale = _mod_vec(modp_ref, mods_ref, 1)
    y = _per_seq(_rms(x) * g_ref[...]) * (1 + scale) + shift
    return y.reshape(x.shape).astype(BF16)


def _gated(x, gate, y):
    return (_per_seq(x) + gate * _per_seq(y)).reshape(x.shape)


def _dot(a, b):
    return jnp.dot(a, b, preferred_element_type=F32)


def _dot_nt(a, b):
    return lax.dot_general(a, b, (((1,), (1,)), ((), ())), preferred_element_type=F32)


def _mod_kernel(c_ref, w_ref, b_ref, o_ref):
    c = c_ref[...]
    a = (c * jax.nn.sigmoid(c)).astype(BF16)
    for t in range(3):
        cols = slice(t * D_MODEL, (t + 1) * D_MODEL)
        o_ref[0, t] = _dot(a, w_ref[0, :, cols].astype(BF16)) + b_ref[0, :, cols]


def _ada_modulation(c_all, ada_w, ada_b):
    n = c_all.shape[0]
    width = N_SUB * 3 * D_MODEL
    return pl.pallas_call(
        _mod_kernel,
        grid=(DEPTH, N_SUB),
        in_specs=[
            pl.BlockSpec((n, D_MODEL), lambda l, j: (0, 0)),
            pl.BlockSpec((1, D_MODEL, 3 * D_MODEL), lambda l, j: (l, 0, j)),
            pl.BlockSpec((1, 1, 3 * D_MODEL), lambda l, j: (l, 0, j)),
        ],
        out_specs=pl.BlockSpec((1, 3, n, D_MODEL), lambda l, j: (l, j, 0, 0)),
        out_shape=jax.ShapeDtypeStruct((DEPTH, N_SUB * 3, n, D_MODEL), F32),
        compiler_params=_params(("arbitrary", "arbitrary")),
        name="ada_modulation",
    )(c_all, ada_w, ada_b.reshape(DEPTH, 1, width))


def _ffn_kernel(n_x, final, layer, k, *refs):
    x_refs, refs = refs[:n_x], refs[n_x:]
    modp_ref, mods_ref, g_ref, win_hbm, wo_hbm = refs[:5]
    refs = refs[5:]
    h_s, a_s, win_s, wo_s, stin_s, sto_s, sem = refs[-7:]
    is_p = pl.program_id(0) < N_PTILES

    @pl.when(pl.program_id(0) == 0)
    def _():
        def in_copy(c):
            return pltpu.make_async_copy(
                win_hbm.at[layer, k, :, pl.ds(c * W_IN_COLS, W_IN_COLS)],
                stin_s.at[c % 2], sem.at[0, c % 2])

        def out_copy(c):
            return pltpu.make_async_copy(
                wo_hbm.at[layer, k, pl.ds(c * W_OUT_ROWS, W_OUT_ROWS), :],
                sto_s.at[c % 2], sem.at[1, c % 2])

        in_copy(0).start()
        out_copy(0).start()
        for c in range(N_W_CHUNKS):
            if c + 1 < N_W_CHUNKS:
                in_copy(c + 1).start()
                out_copy(c + 1).start()
            in_copy(c).wait()
            win_s[:, c * W_IN_COLS:(c + 1) * W_IN_COLS] = stin_s[c % 2].astype(BF16)
            out_copy(c).wait()
            wo_s[c * W_OUT_ROWS:(c + 1) * W_OUT_ROWS, :] = sto_s[c % 2].astype(BF16)

    x = x_refs[0][...] if n_x == 1 else jnp.where(is_p, x_refs[0][...], x_refs[1][...])
    HM = TM // 2
    PIECE = 32
    shift = _mod_vec(modp_ref, mods_ref, 0)
    scale = _mod_vec(modp_ref, mods_ref, 1)

    def prologue_rows(r0, r1):
        xs_ = x[r0:r1]
        y_ = (_rms(xs_) * g_ref[...]).reshape((r1 - r0) // DEC_SEQ, DEC_SEQ, D_MODEL)
        y_ = y_ * (1 + scale[r0 // DEC_SEQ:r1 // DEC_SEQ]) + shift[r0 // DEC_SEQ:r1 // DEC_SEQ]
        return y_.reshape(r1 - r0, D_MODEL).astype(BF16)

    h_s[0:HM] = prologue_rows(0, HM)

    def zero_of(v):
        u = pltpu.bitcast(v, jnp.uint32)
        acc = u[0:8, 0:128]
        for r in range(0, PIECE // 2, 8):
            for cc in range(0, D_MODEL, 128):
                if r or cc:
                    acc = acc | u[r:r + 8, cc:cc + 128]
        return ((acc >> 16) >> 16).astype(F32)

    for c in range(N_FFN_CHUNKS):
        h = h_s[0:HM]
        gate = _dot(h, win_s[:, c * FFN_CHUNK:(c + 1) * FFN_CHUNK])
        up = _dot(h, win_s[:, FFN_DIM + c * FFN_CHUNK:FFN_DIM + (c + 1) * FFN_CHUNK])
        act = gate * jax.nn.sigmoid(gate) * up
        if c < HM // PIECE:
            piece = prologue_rows(HM + c * PIECE, HM + (c + 1) * PIECE)
            h_s[HM + c * PIECE:HM + (c + 1) * PIECE] = piece
            act = jnp.concatenate([jnp.concatenate([act[:8, :128] + zero_of(piece), act[:8, 128:]], axis=1), act[8:]], axis=0)
        a_s[0:HM, c * FFN_CHUNK:(c + 1) * FFN_CHUNK] = act.astype(BF16)
    for c in range(N_FFN_CHUNKS):
        h = h_s[HM:TM]
        gate = _dot(h, win_s[:, c * FFN_CHUNK:(c + 1) * FFN_CHUNK])
        up = _dot(h, win_s[:, FFN_DIM + c * FFN_CHUNK:FFN_DIM + (c + 1) * FFN_CHUNK])
        a_s[HM:TM, c * FFN_CHUNK:(c + 1) * FFN_CHUNK] = (gate * jax.nn.sigmoid(gate) * up).astype(BF16)
    y = _dot(a_s[...], wo_s[...])
    out = _gated(x, HALF_STEP * _mod_vec(modp_ref, mods_ref, 2), y)
    if not final:
        refs[0][...] = out
        return
    fg_ref, yp_ref, ys_ref = refs[:3]
    normed = _rms(out) * fg_ref[...]

    @pl.when(is_p)
    def _():
        yp_ref[...] = normed

    @pl.when(jnp.logical_not(is_p))
    def _():
        ys_ref[...] = normed


def _row_spec(width=D_MODEL):
    return pl.BlockSpec((TM, width), lambda i: (i, 0))


def _prow_spec(width=D_MODEL):
    return pl.BlockSpec((TM, width), lambda i: (_ptile(i), 0))


def _srow_spec(width=D_MODEL):
    return pl.BlockSpec((TM, width), lambda i: (_stile(i), 0))


def _ffn_sublayer(xs, modp, mods, g, w_in, w_out, layer, k, final_g=None):
    final = final_g is not None
    x_specs = [_row_spec()] if len(xs) == 1 else [_prow_spec(), _srow_spec()]
    in_specs = x_specs + _mod_specs() + [
        _resident((1, D_MODEL)),
        pl.BlockSpec(memory_space=pl.ANY),
        pl.BlockSpec(memory_space=pl.ANY),
    ]
    args = list(xs) + [modp, mods, g, w_in, w_out]
    if final:
        in_specs.append(_resident((1, D_MODEL)))
        args.append(final_g)
        out_specs = [_prow_spec(), _srow_spec()]
        out_shape = [jax.ShapeDtypeStruct((P_ROWS, D_MODEL), F32),
                     jax.ShapeDtypeStruct((S_ROWS, D_MODEL), F32)]
    else:
        out_specs = _row_spec()
        out_shape = jax.ShapeDtypeStruct((ROWS, D_MODEL), F32)
    return pl.pallas_call(
        functools.partial(_ffn_kernel, len(xs), final, layer, k),
        grid=(N_TILES,),
        in_specs=in_specs,
        out_specs=out_specs,
        out_shape=out_shape,
        scratch_shapes=[
            pltpu.VMEM((TM, D_MODEL), BF16), pltpu.VMEM((TM, FFN_DIM), BF16),
            pltpu.VMEM((D_MODEL, 2 * FFN_DIM), BF16), pltpu.VMEM((FFN_DIM, D_MODEL), BF16),
            pltpu.VMEM((2, D_MODEL, W_IN_COLS), F32), pltpu.VMEM((2, W_OUT_ROWS, D_MODEL), F32),
            pltpu.SemaphoreType.DMA((2, 2)),
        ],
        compiler_params=_params(("arbitrary",)),
        name="ffn_sublayer",
    )(*args)


def _rope(t, table):
    w = t * table
    return w + pltpu.roll(w, ROPE_DIM, axis=1)


def _mla_proj_kernel(x_ref, modp_ref, mods_ref, g_ref, win_ref, qn_ref, wuq_ref, kvn_ref,
                     wukt_ref, wuk_ref, wuvt_ref, tabp_ref, tabs_ref,
                     latp_ref, krp_ref, lats_ref, krs_ref, qp_ref, kp_ref, vt_ref, qs_ref):
    is_p = pl.program_id(0) < N_PTILES
    h = _modulated(x_ref[...], g_ref, modp_ref, mods_ref)
    z = _dot(h, win_ref[...])
    table = jnp.where(is_p, tabp_ref[...], tabs_ref[...])
    k_rope = _rope(z[:, Q_RANK + KV_RANK:], table)[:, :ROPE_DIM]
    latent = _rms(z[:, Q_RANK:Q_RANK + KV_RANK]) * kvn_ref[...]
    qn = (_rms(z[:, :Q_RANK]) * qn_ref[...]).astype(BF16)
    q_all = _dot(qn, wuq_ref[...])

    def q_rope(hd):
        t = q_all[:, hd * Q_HEAD_COLS + NOPE_DIM:(hd + 1) * Q_HEAD_COLS]
        return (_rope(t, table)[:, :ROPE_DIM] * Q_SCALE).astype(BF16)

    @pl.when(is_p)
    def _():
        latp_ref[...] = latent
        krp_ref[...] = k_rope
        lat_b = latent.astype(BF16)
        k_rope_b = k_rope.astype(BF16)
        k_nope = _dot(lat_b, wuk_ref[...])
        v_t = _dot_nt(wuvt_ref[...], lat_b).astype(BF16)
        for kb in range(TM // TK):
            vt_ref[kb] = v_t[:, kb * TK:(kb + 1) * TK]
        for hd in range(N_HEADS):
            q_nope = q_all[:, hd * Q_HEAD_COLS:hd * Q_HEAD_COLS + NOPE_DIM]
            qp_ref[hd, :, :NOPE_DIM] = (q_nope * Q_SCALE).astype(BF16)
            qp_ref[hd, :, NOPE_DIM:] = q_rope(hd)
            kp_ref[hd, :, :NOPE_DIM] = k_nope[:, hd * NOPE_DIM:(hd + 1) * NOPE_DIM].astype(BF16)
            kp_ref[hd, :, NOPE_DIM:] = k_rope_b

    @pl.when(jnp.logical_not(is_p))
    def _():
        lats_ref[...] = latent
        krs_ref[...] = k_rope
        for hd in range(N_HEADS):
            q_nope = q_all[:, hd * Q_HEAD_COLS:hd * Q_HEAD_COLS + NOPE_DIM].astype(BF16)
            q_lat = _dot(q_nope, wukt_ref[hd]) * Q_SCALE
            qs_ref[hd, :, :KV_RANK] = q_lat.astype(BF16)
            qs_ref[hd, :, KV_RANK:] = q_rope(hd)


def _mla_project(x, modp, mods, g, w_in_ext, q_norm, w_uq_ext, kv_norm, w_uk_t, w_uk_flat,
                 w_uv_t, table_p, table_s):
    return pl.pallas_call(
        _mla_proj_kernel,
        grid=(N_TILES,),
        in_specs=[_row_spec()] + _mod_specs() + [
            _resident((1, D_MODEL)),
            _resident(w_in_ext.shape),
            _resident((1, Q_RANK)),
            _resident(w_uq_ext.shape),
            _resident((1, KV_RANK)),
            _resident(w_uk_t.shape),
            _resident(w_uk_flat.shape),
            _resident(w_uv_t.shape),
            pl.BlockSpec((TM, 2 * ROPE_DIM), lambda i: (i % TILES_PER_SEQ, 0)),
            _resident((TM, 2 * ROPE_DIM)),
        ],
        out_specs=[
            _prow_spec(KV_RANK),
            _prow_spec(ROPE_DIM),
            _srow_spec(KV_RANK),
            _srow_spec(ROPE_DIM),
            pl.BlockSpec((N_HEADS, TM, HEAD_DIM), lambda i: (0, _ptile(i), 0)),
            pl.BlockSpec((N_HEADS, TM, HEAD_DIM), lambda i: (0, _ptile(i), 0)),
            pl.BlockSpec((TM // TK, N_HEADS * V_DIM, TK), lambda i: (_ptile(i), 0, 0)),
            pl.BlockSpec((N_HEADS, TM, QK_DIM), lambda i: (0, _stile(i), 0)),
        ],
        out_shape=[
            jax.ShapeDtypeStruct((P_ROWS, KV_RANK), F32),
            jax.ShapeDtypeStruct((P_ROWS, ROPE_DIM), F32),
            jax.ShapeDtypeStruct((S_ROWS, KV_RANK), F32),
            jax.ShapeDtypeStruct((S_ROWS, ROPE_DIM), F32),
            jax.ShapeDtypeStruct((N_HEADS, P_ROWS, HEAD_DIM), BF16),
            jax.ShapeDtypeStruct((N_HEADS, P_ROWS, HEAD_DIM), BF16),
            jax.ShapeDtypeStruct((P_ROWS // TK, N_HEADS * V_DIM, TK), BF16),
            jax.ShapeDtypeStruct((N_HEADS, S_ROWS, QK_DIM), BF16),
        ],
        compiler_params=_params(("arbitrary",)),
        name="mla_project",
    )(x, modp, mods, g, w_in_ext, q_norm, w_uq_ext, kv_norm, w_uk_t, w_uk_flat, w_uv_t,
      table_p, table_s)


def _attn_prompt_kernel(q_ref, k_ref, vt_ref, o_ref, m_s, l_s, acc_s, a_s, p_s):
    qi = pl.program_id(1)
    m_s[...] = jnp.full_like(m_s, -jnp.inf)
    l_s[...] = jnp.zeros_like(l_s)
    acc_s[...] = jnp.zeros_like(acc_s)

    def step(kb0, n_kb, masked):
        keys = n_kb * TK
        off = pl.multiple_of(kb0 * TK, TK)
        for hd in range(N_HEADS):
            s = _dot_nt(k_ref[hd, pl.ds(off, keys), :], q_ref[hd])
            if masked:
                k_pos = off + lax.broadcasted_iota(jnp.int32, (keys, TQ), 0)
                q_pos = qi * TQ + lax.broadcasted_iota(jnp.int32, (keys, TQ), 1)
                s = jnp.where(k_pos <= q_pos, s, -jnp.inf)
            m_old = m_s[hd]
            m_new = jnp.maximum(m_old, jnp.max(s, axis=0, keepdims=True))
            alpha = jnp.exp2(m_old - m_new)
            p = jnp.exp2(s - m_new)
            l_s[hd] = alpha * l_s[hd] + jnp.sum(p, axis=0, keepdims=True)
            p_s[hd, :keys] = p.astype(BF16)
            a_s[hd] = alpha
            m_s[hd] = m_new
        for hd in range(N_HEADS):
            pv = _dot(vt_ref[kb0, hd * V_DIM:(hd + 1) * V_DIM, :], p_s[hd, :TK])
            for j in range(1, n_kb):
                pv += _dot(vt_ref[kb0 + j, hd * V_DIM:(hd + 1) * V_DIM, :],
                           p_s[hd, j * TK:(j + 1) * TK])
            acc_s[hd] = a_s[hd] * acc_s[hd] + pv

    def body(j, carry):
        step(2 * j, 2, False)
        return carry

    lax.fori_loop(0, qi // 2, body, 0)

    @pl.when(qi % 2 == 1)
    def _():
        step(qi - 1, 2, True)

    @pl.when(qi % 2 == 0)
    def _():
        step(qi, 1, True)
    for hd in range(N_HEADS):
        o_t = acc_s[hd] / l_s[hd]
        o_ref[:, hd * V_DIM:(hd + 1) * V_DIM] = o_t.T.astype(BF16)


def _attn_prompt(q_p, k_p, v_t):
    nq = SEQ // TQ
    return pl.pallas_call(
        _attn_prompt_kernel,
        grid=(BATCH, nq),
        in_specs=[
            pl.BlockSpec((N_HEADS, TQ, HEAD_DIM), lambda b, i: (0, b * nq + i, 0)),
            pl.BlockSpec((N_HEADS, SEQ, HEAD_DIM), lambda b, i: (0, b, 0)),
            pl.BlockSpec((N_KBLOCKS, N_HEADS * V_DIM, TK), lambda b, i: (b, 0, 0)),
        ],
        out_specs=pl.BlockSpec((TQ, N_HEADS * V_DIM), lambda b, i: (b * nq + i, 0)),
        out_shape=jax.ShapeDtypeStruct((P_ROWS, N_HEADS * V_DIM), BF16),
        scratch_shapes=[pltpu.VMEM((N_HEADS, 1, TQ), F32), pltpu.VMEM((N_HEADS, 1, TQ), F32),
                        pltpu.VMEM((N_HEADS, V_DIM, TQ), F32), pltpu.VMEM((N_HEADS, 1, TQ), F32),
                        pltpu.VMEM((N_HEADS, 2 * TK, TQ), BF16)],
        compiler_params=_params(("arbitrary", "arbitrary")),
        name="attn_prompt",
    )(q_p, k_p, v_t)


def _attn_sample_kernel(layer, pt_ref, q_ref, ln_ref, rn_ref, lat_hbm, rope_hbm, o_ref,
                        lat_buf, rope_buf, kl_s, kr_s, sem):
    b = pl.program_id(0)
    slot = lax.rem(b, PAGE_SLOTS)
    rows = N_HEADS * DEC_SEQ

    def page_copies(page, sl, p):
        return (pltpu.make_async_copy(lat_hbm.at[layer, page], lat_buf.at[sl, p], sem.at[sl, 0]),
                pltpu.make_async_copy(rope_hbm.at[layer, page], rope_buf.at[sl, p], sem.at[sl, 1]))

    def start_pages(seq, sl):
        def body(p, carry):
            for cp in page_copies(pt_ref[seq, p], sl, p):
                cp.start()
            return carry
        lax.fori_loop(0, N_PAGES, body, 0, unroll=True)

    @pl.when(b == 0)
    def _():
        for ahead in range(PAGE_SLOTS - 1):
            start_pages(ahead, ahead)

    @pl.when(b + PAGE_SLOTS - 1 < DEC_BATCH)
    def _():
        start_pages(b + PAGE_SLOTS - 1, lax.rem(b + PAGE_SLOTS - 1, PAGE_SLOTS))

    for p in range(N_PAGES):
        for cp in page_copies(0, slot, p):
            cp.wait()

    for p in range(N_PAGES):
        kl_s[p * PAGE_SIZE:(p + 1) * PAGE_SIZE, :] = lat_buf[slot, p].astype(BF16)
        kr_s[:, p * PAGE_SIZE:(p + 1) * PAGE_SIZE] = rope_buf[slot, p].astype(BF16)

    q = q_ref[0]
    q_lat = q[:, :KV_RANK]
    q_rope = q[:, KV_RANK:]
    k_lat = kl_s[...]
    s = _dot_nt(q_lat, k_lat) + _dot(q_rope, kr_s[...])
    new_lat = ln_ref[...].astype(BF16)
    new_rope = rn_ref[...].astype(BF16)
    s_new = _dot_nt(q_lat, new_lat) + _dot_nt(q_rope, new_rope)
    q_pos = lax.broadcasted_iota(jnp.int32, (rows, DEC_SEQ), 0) & (DEC_SEQ - 1)
    k_pos = lax.broadcasted_iota(jnp.int32, (rows, DEC_SEQ), 1)
    s_new = jnp.where(k_pos <= q_pos, s_new, -jnp.inf)
    m = jnp.maximum(jnp.max(s, axis=-1, keepdims=True), jnp.max(s_new, axis=-1, keepdims=True))
    p = jnp.exp2(s - m)
    p_new = jnp.exp2(s_new - m)
    l = jnp.sum(p, axis=-1, keepdims=True) + jnp.sum(p_new, axis=-1, keepdims=True)
    pb = p.astype(BF16)
    half = KV_RANK // 2
    ctx = jnp.concatenate([_dot(pb, kl_s[:, :half]), _dot(pb, kl_s[:, half:])], axis=1)
    ctx = (ctx + _dot(p_new.astype(BF16), new_lat)) / l
    for hd in range(N_HEADS):
        o_ref[0, :, hd * KV_RANK:(hd + 1) * KV_RANK] = ctx[hd * DEC_SEQ:(hd + 1) * DEC_SEQ]


def _attn_sample(layer, page_table, q_s, lat, k_rope, cache_lat, cache_rope_t):
    rows = N_HEADS * DEC_SEQ
    return pl.pallas_call(
        functools.partial(_attn_sample_kernel, layer),
        grid_spec=pltpu.PrefetchScalarGridSpec(
            num_scalar_prefetch=1,
            grid=(DEC_BATCH,),
            in_specs=[
                pl.BlockSpec((1, rows, QK_DIM), lambda b, pt: (b, 0, 0)),
                pl.BlockSpec((DEC_SEQ, KV_RANK), lambda b, pt: (b, 0)),
                pl.BlockSpec((DEC_SEQ, ROPE_DIM), lambda b, pt: (b, 0)),
                pl.BlockSpec(memory_space=pl.ANY),
                pl.BlockSpec(memory_space=pl.ANY),
            ],
            out_specs=pl.BlockSpec((1, DEC_SEQ, N_HEADS * KV_RANK), lambda b, pt: (b, 0, 0)),
            scratch_shapes=[
                pltpu.VMEM((PAGE_SLOTS, N_PAGES, PAGE_SIZE, KV_RANK), F32),
                pltpu.VMEM((PAGE_SLOTS, N_PAGES, ROPE_DIM, PAGE_SIZE), F32),
                pltpu.VMEM((PAST_LEN, KV_RANK), BF16),
                pltpu.VMEM((ROPE_DIM, PAST_LEN), BF16),
                pltpu.SemaphoreType.DMA((PAGE_SLOTS, 2)),
            ],
        ),
        out_shape=jax.ShapeDtypeStruct((DEC_BATCH, DEC_SEQ, N_HEADS * KV_RANK), F32),
        compiler_params=_params(("arbitrary",)),
        name="attn_sample",
    )(page_table, q_s, lat, k_rope, cache_lat, cache_rope_t)


def _mla_out_kernel(x_ref, modp_ref, mods_ref, op_ref, cs_ref, wuv_ref, wo_ref, o_ref, o_s):
    is_p = pl.program_id(0) < N_PTILES

    @pl.when(is_p)
    def _():
        o_s[...] = op_ref[...]

    @pl.when(jnp.logical_not(is_p))
    def _():
        for hd in range(N_HEADS):
            ctx = cs_ref[:, hd * KV_RANK:(hd + 1) * KV_RANK].astype(BF16)
            o_s[:, hd * V_DIM:(hd + 1) * V_DIM] = _dot(ctx, wuv_ref[hd]).astype(BF16)

    y = _dot(o_s[...], wo_ref[...])
    o_ref[...] = _gated(x_ref[...], _mod_vec(modp_ref, mods_ref, 2), y)


def _mla_output(x, modp, mods, o_p, ctx_s, w_uv_h, w_o):
    row = pl.BlockSpec((TM, D_MODEL), lambda i: (i, 0))
    return pl.pallas_call(
        _mla_out_kernel,
        grid=(N_TILES,),
        in_specs=[row] + _mod_specs() + [
            pl.BlockSpec((TM, N_HEADS * V_DIM), lambda i: (_ptile(i), 0)),
            pl.BlockSpec((TM, N_HEADS * KV_RANK), lambda i: (_stile(i), 0)),
            _resident(w_uv_h.shape),
            _resident(w_o.shape),
        ],
        out_specs=row,
        out_shape=jax.ShapeDtypeStruct((ROWS, D_MODEL), F32),
        scratch_shapes=[pltpu.VMEM((TM, N_HEADS * V_DIM), BF16)],
        compiler_params=_params(("arbitrary",)),
        name="mla_output",
    )(x, modp, mods, o_p, ctx_s, w_uv_h, w_o)


def _gmlp_kernel(x_ref, modp_ref, mods_ref, g_ref, win_ref, bin_ref, lng_ref, lnb_ref,
                 wmp_ref, wms_ref, bp_ref, bs_ref, wout_ref, o_ref, vp_ref, vs_ref, vb_s, t_s):
    i = pl.program_id(0)
    is_p = i < N_PTILES
    x = x_ref[...]
    h = _modulated(x, g_ref, modp_ref, mods_ref)
    z = _dot(h, win_ref[...]) + bin_ref[...]
    z = z / (1 + jnp.exp2(z * (GELU_A + GELU_B * (z * z))))
    v = z[:, GMLP_DIM:]
    mu = jnp.mean(v, axis=-1, keepdims=True)
    var = jnp.mean(jnp.square(v - mu), axis=-1, keepdims=True)
    vn = (v - mu) * lax.rsqrt(var + EPS) * lng_ref[...] + lnb_ref[...]

    @pl.when(jnp.logical_and(is_p, i % TILES_PER_SEQ == TILES_PER_SEQ - 1))
    def _():
        vp_ref[...] = vn[TM - CHUNK:]

    @pl.when(jnp.logical_not(is_p))
    def _():
        vs_ref[...] = vn

    vb_s[...] = vn.astype(BF16)
    r = lax.broadcasted_iota(jnp.int32, (CHUNK, CHUNK), 0)
    c = lax.broadcasted_iota(jnp.int32, (CHUNK, CHUNK), 1)
    same_seq = jnp.logical_or(is_p, jnp.bitwise_xor(r, c) < DEC_SEQ)
    keep = jnp.logical_and(r >= c, same_seq)
    bias = jnp.where(is_p, bp_ref[...], bs_ref[...])
    for g in range(GMLP_GROUPS):
        cols = slice(g * GROUP_DIM, (g + 1) * GROUP_DIM)
        w = jnp.where(keep, jnp.where(is_p, wmp_ref[g], wms_ref[g]), 0.0).astype(BF16)
        n_ch = TM // CHUNK
        v_g = jnp.concatenate([vb_s[ch * CHUNK:(ch + 1) * CHUNK, cols] for ch in range(n_ch)], axis=1)
        mixed = _dot(w, v_g)
        for ch in range(n_ch):
            rws = slice(ch * CHUNK, (ch + 1) * CHUNK)
            m_ch = mixed[:, ch * GROUP_DIM:(ch + 1) * GROUP_DIM] + bias[:, cols]
            t_s[rws, cols] = (z[rws, cols] * m_ch).astype(BF16)
    y = _dot(t_s[...], wout_ref[...])
    o_ref[...] = _gated(x, _mod_vec(modp_ref, mods_ref, 2), y)


def _gmlp_sublayer(x, modp, mods, g, w_in, b_in, ln_g, ln_b, wm_p, wm_s, bias_p, bias_s, w_out):
    row = pl.BlockSpec((TM, D_MODEL), lambda i: (i, 0))
    return pl.pallas_call(
        _gmlp_kernel,
        grid=(N_TILES,),
        in_specs=[row] + _mod_specs() + [
            _resident((1, D_MODEL)),
            _resident(w_in.shape),
            _resident((1, 2 * GMLP_DIM)),
            _resident((1, GMLP_DIM)),
            _resident((1, GMLP_DIM)),
            _resident(wm_p.shape),
            _resident(wm_s.shape),
            _resident(bias_p.shape),
            _resident(bias_s.shape),
            _resident(w_out.shape),
        ],
        out_specs=[
            row,
            pl.BlockSpec((CHUNK, GMLP_DIM),
                         lambda i: (jnp.minimum(i // TILES_PER_SEQ, BATCH - 1), 0)),
            _srow_spec(GMLP_DIM),
        ],
        out_shape=[jax.ShapeDtypeStruct((ROWS, D_MODEL), F32),
                   jax.ShapeDtypeStruct((BATCH * CHUNK, GMLP_DIM), F32),
                   jax.ShapeDtypeStruct((S_ROWS, GMLP_DIM), F32)],
        scratch_shapes=[pltpu.VMEM((TM, GMLP_DIM), BF16), pltpu.VMEM((TM, GMLP_DIM), BF16)],
        compiler_params=_params(("arbitrary",)),
        name="gmlp_sublayer",
    )(x, modp, mods, g, w_in, b_in, ln_g, ln_b, wm_p, wm_s, bias_p, bias_s, w_out)


def _rope_table(pos):
    half = ROPE_DIM // 2
    freqs = jnp.power(ROPE_THETA, -jnp.arange(half, dtype=F32) / half)
    ang = pos[:, None] * freqs[None, :]
    cos, sin = jnp.cos(ang), jnp.sin(ang)
    return jnp.concatenate([cos, cos, -sin, sin], axis=-1)


def _with_rotated(w):
    half = ROPE_DIM // 2
    return jnp.concatenate([w, w[..., -half:], w[..., -ROPE_DIM:-half]], axis=-1)


def _split_mods(mod_l):
    out = []
    for k in range(N_SUB):
        m = mod_l[3 * k:3 * k + 3]
        out.append((m[:, :BATCH], m[:, BATCH:, None, :]))
    return out


def kernel(x_prompt, x_sample, cache_kv_latent, cache_k_rope, page_table, c_prompt, c_sample,
           ada_w, ada_b, norm_g, ffn_w_in, ffn_w_out,
           a_w_in, a_q_norm, a_w_uq, a_kv_norm, a_w_uk, a_w_uv, a_w_o,
           b_w_in, b_b_in, b_ln_g, b_ln_b, b_w_s, b_b_s, b_w_out, final_g):
    xs = (x_prompt.reshape(P_ROWS, D_MODEL), x_sample.reshape(S_ROWS, D_MODEL))
    c_all = jnp.concatenate([c_prompt, c_sample])
    mod = _ada_modulation(c_all, ada_w, ada_b)
    table_p = _rope_table(jnp.arange(SEQ, dtype=F32))
    table_s = _rope_table(jnp.tile(jnp.arange(DEC_SEQ, dtype=F32) + PAST_LEN, TM // DEC_SEQ))
    cache_rope_t = cache_k_rope.transpose(0, 1, 3, 2)

    lat_p, rope_p, lat_s, rope_s, v_p, v_s = [], [], [], [], [], []
    for i in range(DEPTH):
        mods_i = _split_mods(mod[i])
        j = i // 2
        x = _ffn_sublayer(xs if i == 0 else (x,), *mods_i[0], norm_g[i, 0][None], ffn_w_in, ffn_w_out, i, 0)
        if i % 2 == 0:
            w_in_ext = _with_rotated(a_w_in[j]).astype(BF16)
            w_uq_ext = _with_rotated(a_w_uq[j]).reshape(Q_RANK, N_HEADS * Q_HEAD_COLS).astype(BF16)
            w_uk_t = a_w_uk[j].transpose(1, 2, 0).astype(BF16)
            w_uk_flat = a_w_uk[j].reshape(KV_RANK, N_HEADS * NOPE_DIM).astype(BF16)
            w_uv_h = a_w_uv[j].transpose(1, 0, 2).astype(BF16)
            w_uv_t = a_w_uv[j].reshape(KV_RANK, N_HEADS * V_DIM).T.astype(BF16)
            l_p, r_p, l_s, r_s, q_p, k_p, v_t, q_s = _mla_project(
                x, *mods_i[1], norm_g[i, 1][None], w_in_ext, a_q_norm[j][None], w_uq_ext,
                a_kv_norm[j][None], w_uk_t, w_uk_flat, w_uv_t, table_p, table_s)
            o_p = _attn_prompt(q_p, k_p, v_t)
            q_s = q_s.reshape(N_HEADS, DEC_BATCH, DEC_SEQ, QK_DIM)
            q_s = q_s.transpose(1, 0, 2, 3).reshape(DEC_BATCH, N_HEADS * DEC_SEQ, QK_DIM)
            ctx_s = _attn_sample(j, page_table, q_s, l_s, r_s, cache_kv_latent, cache_rope_t)
            x = _mla_output(x, *mods_i[1], o_p, ctx_s.reshape(S_ROWS, N_HEADS * KV_RANK),
                            w_uv_h, a_w_o[j].astype(BF16))
            lat_p.append(l_p.reshape(BATCH, SEQ, KV_RANK))
            rope_p.append(r_p.reshape(BATCH, SEQ, ROPE_DIM))
            lat_s.append(l_s.reshape(DEC_BATCH, DEC_SEQ, KV_RANK))
            rope_s.append(r_s.reshape(DEC_BATCH, DEC_SEQ, ROPE_DIM))
        else:
            wm_s = jnp.tile(b_w_s[j][:, :DEC_SEQ, :DEC_SEQ], (1, CHUNK // DEC_SEQ, CHUNK // DEC_SEQ))
            bias_p = jnp.repeat(b_b_s[j].T, GROUP_DIM, axis=1)
            bias_s = jnp.tile(bias_p[:DEC_SEQ], (CHUNK // DEC_SEQ, 1))
            x, vp, vs = _gmlp_sublayer(
                x, *mods_i[1], norm_g[i, 1][None], b_w_in[j].astype(BF16), b_b_in[j][None],
                b_ln_g[j][None], b_ln_b[j][None], b_w_s[j], wm_s, bias_p, bias_s,
                b_w_out[j].astype(BF16))
            v_p.append(vp.reshape(BATCH, CHUNK, GMLP_DIM))
            v_s.append(vs.reshape(DEC_BATCH, DEC_SEQ, GMLP_DIM))
        x = _ffn_sublayer((x,), *mods_i[2], norm_g[i, 2][None], ffn_w_in, ffn_w_out, i, 1,
                          final_g=final_g[None] if i == DEPTH - 1 else None)

    y_p, y_s = x
    return (
        y_p.reshape(BATCH, SEQ, D_MODEL),
        y_s.reshape(DEC_BATCH, DEC_SEQ, D_MODEL),
        jnp.stack(lat_p), jnp.stack(rope_p), jnp.stack(lat_s), jnp.stack(rope_s),
        jnp.stack(v_p), jnp.stack(v_s),
    )
```
